```python
import jax, jax.numpy as jnp
from jax import lax
import numpy as np

D_MODEL = 1024
BATCH = 16
SEQ = 4096
DEPTH = 4

MLA_HEADS = 8
QK_NOPE_DIM = 64
QK_ROPE_DIM = 32
V_HEAD_DIM = 64
Q_LORA_RANK = 256
KV_LORA_RANK = 128
ROPE_THETA = 10000.0
Q_BLOCK = 128

MLSTM_HEADS = 4
MLSTM_QK_DIM = 64
MLSTM_V_DIM = 128
CONV_K = 4
CHUNK = 64

D_FF = 2816
EPS = 1e-6

MLA_WIDTH = MLA_HEADS * V_HEAD_DIM
MLSTM_WIDTH = MLSTM_HEADS * MLSTM_V_DIM
MIX_WIDTH = MLA_WIDTH + MLSTM_WIDTH
MLSTM_QK_WIDTH = MLSTM_HEADS * MLSTM_QK_DIM
Q_UP_WIDTH = MLA_HEADS * (QK_NOPE_DIM + QK_ROPE_DIM)
KV_UP_WIDTH = MLA_HEADS * (QK_NOPE_DIM + V_HEAD_DIM)
IN_SPLITS = (Q_LORA_RANK, KV_LORA_RANK, QK_ROPE_DIM,
             MLSTM_QK_WIDTH, MLSTM_QK_WIDTH, MLSTM_WIDTH, MLSTM_WIDTH,
             MLSTM_HEADS, MLSTM_HEADS)
D_IN = sum(IN_SPLITS)

kernel_name = "hymba_mla_mlstm_macaron"


def rmsnorm(x, g):
    xf = x.astype(jnp.float32)
    y = xf * lax.rsqrt(jnp.mean(xf * xf, axis=-1, keepdims=True) + EPS)
    return (y * g.astype(jnp.float32)).astype(x.dtype)


def head_rmsnorm(x, g):
    B, S, H, d = x.shape
    xf = x.astype(jnp.float32)
    y = xf * lax.rsqrt(jnp.mean(xf * xf, axis=-1, keepdims=True) + EPS)
    return (y.reshape(B, S, H * d) * g.astype(jnp.float32)).astype(x.dtype)


def half_swiglu(x, g, w_gate, w_up, w_down):
    h = rmsnorm(x, g)
    return x + 0.5 * ((jax.nn.silu(h @ w_gate) * (h @ w_up)) @ w_down)


def rope_tables(positions):
    inv = ROPE_THETA ** (-jnp.arange(0, QK_ROPE_DIM, 2, dtype=jnp.float32) / QK_ROPE_DIM)
    ang = positions.astype(jnp.float32)[..., None] * inv
    cos, sin = jnp.cos(ang), jnp.sin(ang)
    return jnp.concatenate([cos, cos], -1), jnp.concatenate([sin, sin], -1)


def apply_rope(x, cos, sin):
    xf = x.astype(jnp.float32)
    x1, x2 = jnp.split(xf, 2, axis=-1)
    rot = jnp.concatenate([-x2, x1], axis=-1)
    return (xf * cos + rot * sin).astype(x.dtype)


def mla_attention(c_q, c_kv, k_rope_raw, cos, sin, q_norm_g, w_uq, kv_norm_g, w_ukv):
    B, S = c_q.shape[:2]
    q = (rmsnorm(c_q, q_norm_g) @ w_uq).reshape(B, S, MLA_HEADS, QK_NOPE_DIM + QK_ROPE_DIM)
    q_nope, q_rope = q[..., :QK_NOPE_DIM], q[..., QK_NOPE_DIM:]
    q_rope = apply_rope(q_rope, cos[:, :, None, :], sin[:, :, None, :])
    kv = (rmsnorm(c_kv, kv_norm_g) @ w_ukv).reshape(B, S, MLA_HEADS, QK_NOPE_DIM + V_HEAD_DIM)
    k_nope, v = kv[..., :QK_NOPE_DIM], kv[..., QK_NOPE_DIM:]
    k_rope = apply_rope(k_rope_raw, cos, sin)
    nb = S // Q_BLOCK
    qn_b = q_nope.reshape(B, nb, Q_BLOCK, MLA_HEADS, QK_NOPE_DIM).swapaxes(0, 1)
    qr_b = q_rope.reshape(B, nb, Q_BLOCK, MLA_HEADS, QK_ROPE_DIM).swapaxes(0, 1)
    key_pos = jnp.arange(S)
    scale = (QK_NOPE_DIM + QK_ROPE_DIM) ** -0.5

    def one_block(args):
        qn, qr, blk = args
        s = (jnp.einsum('bqhd,bkhd->bhqk', qn, k_nope)
             + jnp.einsum('bqhr,bkr->bhqk', qr, k_rope))
        s = s.astype(jnp.float32) * scale
        q_pos = blk * Q_BLOCK + jnp.arange(Q_BLOCK)
        s = jnp.where(key_pos[None, :] <= q_pos[:, None], s, -jnp.inf)
        p = jax.nn.softmax(s, axis=-1).astype(v.dtype)
        return jnp.einsum('bhqk,bkhd->bqhd', p, v)

    o = lax.map(one_block, (qn_b, qr_b, jnp.arange(nb)))
    return o.swapaxes(0, 1).reshape(B, S, MLA_HEADS, V_HEAD_DIM)


def causal_depthwise_conv(u, w, b):
    C = u.shape[-1]
    y = lax.conv_general_dilated(u, w[:, None, :].astype(u.dtype), window_strides=(1,),
                                 padding=[(CONV_K - 1, 0)],
                                 dimension_numbers=('NWC', 'WIO', 'NWC'),
                                 feature_group_count=C)
    return y + b


def mlstm_chunkwise(q, k, v, i_pre, log_f):
    B, S, H, dk = q.shape
    dv = v.shape[-1]
    nc, L = S // CHUNK, CHUNK
    f32 = jnp.float32

    def chunks(t):
        return t.astype(f32).reshape(B, nc, L, H, -1).transpose(0, 3, 1, 2, 4)

    qc = chunks(q) * (dk ** -0.5)
    kc, vc = chunks(k), chunks(v)
    ig = i_pre.astype(f32).reshape(B, nc, L, H).transpose(0, 3, 1, 2)
    lf = log_f.astype(f32).reshape(B, nc, L, H).transpose(0, 3, 1, 2)
    b = jnp.cumsum(lf, axis=-1)
    b_end = b[..., -1]

    w_end = b_end[..., None] - b + ig
    g = jnp.max(w_end, axis=-1)
    e_end = jnp.exp(w_end - g[..., None])
    kv_chunk = jnp.einsum('bhcl,bhcld,bhcle->bhcde', e_end, kc, vc)
    n_chunk = jnp.einsum('bhcl,bhcld->bhcd', e_end, kc)

    def step(carry, xs):
        C, n, m = carry
        a, gc, kvc, nck = xs
        m_new = jnp.maximum(a + m, gc)
        d_old = jnp.exp(a + m - m_new)
        d_new = jnp.exp(gc - m_new)
        C_new = d_old[..., None, None] * C + d_new[..., None, None] * kvc
        n_new = d_old[..., None] * n + d_new[..., None] * nck
        return (C_new, n_new, m_new), (C, n, m)

    init = (jnp.zeros((B, H, dk, dv), f32), jnp.zeros((B, H, dk), f32), jnp.zeros((B, H), f32))
    xs = (jnp.moveaxis(b_end, 2, 0), jnp.moveaxis(g, 2, 0),
          jnp.moveaxis(kv_chunk, 2, 0), jnp.moveaxis(n_chunk, 2, 0))
    _, (C_prev, n_prev, m_prev) = lax.scan(step, init, xs)
    C_prev = jnp.moveaxis(C_prev, 0, 2)
    n_prev = jnp.moveaxis(n_prev, 0, 2)
    m_prev = jnp.moveaxis(m_prev, 0, 2)

    causal = jnp.tril(jnp.ones((L, L), dtype=bool))
    log_d = jnp.where(causal, b[..., :, None] - b[..., None, :] + ig[..., None, :], -jnp.inf)
    inter = b + m_prev[..., None]
    m_t = jnp.maximum(inter, jnp.max(log_d, axis=-1))
    p = jnp.exp(log_d - m_t[..., None]) * jnp.einsum('bhcld,bhcsd->bhcls', qc, kc)
    e_inter = jnp.exp(inter - m_t)
    num = (e_inter[..., None] * jnp.einsum('bhcld,bhcde->bhcle', qc, C_prev)
           + jnp.einsum('bhcls,bhcse->bhcle', p, vc))
    den = e_inter * jnp.einsum('bhcld,bhcd->bhcl', qc, n_prev) + jnp.sum(p, axis=-1)
    h = num / jnp.maximum(jnp.abs(den), jnp.exp(-m_t))[..., None]
    return h.transpose(0, 2, 3, 1, 4).reshape(B, S, H, dv).astype(v.dtype)


def setup_inputs(seed: int = 0) -> dict:
    key = jax.random.key(seed)
    ks = jax.random.split(key, 32)
    f32 = jnp.float32

    def w(k, shape, fan_in):
        return jax.random.normal(k, shape, f32) * (fan_in ** -0.5)

    def gain(k, shape):
        return 1.0 + 0.02 * jax.random.normal(k, shape, f32)

    x = jax.random.normal(ks[0], (BATCH, SEQ, D_MODEL), f32)
    offsets = jax.random.randint(ks[1], (BATCH, 1), 0, 4096, dtype=jnp.int32)
    positions = (offsets + jnp.arange(SEQ, dtype=jnp.int32)[None, :]).astype(jnp.int32)
    return {
        "x": x,
        "positions": positions,
        "ffn1_norm": gain(ks[2], (DEPTH, D_MODEL)),
        "ffn1_w_gate": w(ks[3], (DEPTH, D_MODEL, D_FF), D_MODEL),
        "ffn1_w_up": w(ks[4], (DEPTH, D_MODEL, D_FF), D_MODEL),
        "ffn1_w_down": w(ks[5], (DEPTH, D_FF, D_MODEL), D_FF),
        "mix_norm": gain(ks[6], (DEPTH, D_MODEL)),
        "w_in": w(ks[7], (DEPTH, D_MODEL, D_IN), D_MODEL),
        "q_latent_norm": gain(ks[8], (DEPTH, Q_LORA_RANK)),
        "w_uq": w(ks[9], (DEPTH, Q_LORA_RANK, Q_UP_WIDTH), Q_LORA_RANK),
        "kv_latent_norm": gain(ks[10], (DEPTH, KV_LORA_RANK)),
        "w_ukv": w(ks[11], (DEPTH, KV_LORA_RANK, KV_UP_WIDTH), KV_LORA_RANK),
        "conv_w": w(ks[12], (DEPTH, CONV_K, 2 * MLSTM_QK_WIDTH), CONV_K),
        "conv_b": 0.01 * jax.random.normal(ks[13], (DEPTH, 2 * MLSTM_QK_WIDTH), f32),
        "b_igate": 0.1 * jax.random.normal(ks[14], (DEPTH, MLSTM_HEADS), f32),
        "b_fgate": (jnp.linspace(3.0, 6.0, MLSTM_HEADS, dtype=f32)[None, :]
                    + 0.1 * jax.random.normal(ks[15], (DEPTH, MLSTM_HEADS), f32)),
        "attn_head_norm": gain(ks[16], (DEPTH, MLA_WIDTH)),
        "mlstm_head_norm": gain(ks[17], (DEPTH, MLSTM_WIDTH)),
        "w_out": w(ks[18], (DEPTH, MIX_WIDTH, D_MODEL), MIX_WIDTH),
        "ffn2_norm": gain(ks[19], (DEPTH, D_MODEL)),
        "ffn2_w_gate": w(ks[20], (DEPTH, D_MODEL, D_FF), D_MODEL),
        "ffn2_w_up": w(ks[21], (DEPTH, D_MODEL, D_FF), D_MODEL),
        "ffn2_w_down": w(ks[22], (DEPTH, D_FF, D_MODEL), D_FF),
        "final_norm": gain(ks[23], (D_MODEL,)),
    }


def reference(x, positions, ffn1_norm, ffn1_w_gate, ffn1_w_up, ffn1_w_down,
              mix_norm, w_in, q_latent_norm, w_uq, kv_latent_norm, w_ukv,
              conv_w, conv_b, b_igate, b_fgate, attn_head_norm, mlstm_head_norm,
              w_out, ffn2_norm, ffn2_w_gate, ffn2_w_up, ffn2_w_down, final_norm):
    B, S, _ = x.shape
    cos, sin = rope_tables(positions)
    split_idx = [int(v) for v in np.cumsum(IN_SPLITS)[:-1]]

    for l in range(DEPTH):
        x = half_swiglu(x, ffn1_norm[l], ffn1_w_gate[l], ffn1_w_up[l], ffn1_w_down[l])

        h = rmsnorm(x, mix_norm[l])
        z = h @ w_in[l]
        c_q, c_kv, k_r, q_m, k_m, v_m, o_m, i_m, f_m = jnp.split(z, split_idx, axis=-1)

        y_att = mla_attention(c_q, c_kv, k_r, cos, sin, q_latent_norm[l], w_uq[l],
                              kv_latent_norm[l], w_ukv[l])
        y_att = head_rmsnorm(y_att, attn_head_norm[l])

        qk = jax.nn.silu(causal_depthwise_conv(jnp.concatenate([q_m, k_m], axis=-1),
                                               conv_w[l], conv_b[l]))
        q_c, k_c = jnp.split(qk, 2, axis=-1)
        i_pre = (i_m + b_igate[l]).astype(jnp.float32)
        log_f = jax.nn.log_sigmoid((f_m + b_fgate[l]).astype(jnp.float32))
        y_mem = mlstm_chunkwise(q_c.reshape(B, S, MLSTM_HEADS, MLSTM_QK_DIM),
                                k_c.reshape(B, S, MLSTM_HEADS, MLSTM_QK_DIM),
                                v_m.reshape(B, S, MLSTM_HEADS, MLSTM_V_DIM),
                                i_pre, log_f)
        y_mem = jax.nn.sigmoid(o_m) * head_rmsnorm(y_mem, mlstm_head_norm[l])

        x = x + jnp.concatenate([y_att, y_mem], axis=-1) @ w_out[l]

        x = half_swiglu(x, ffn2_norm[l], ffn2_w_gate[l], ffn2_w_up[l], ffn2_w_down[l])

    return rmsnorm(x, final_norm)
```

```python
import functools

import jax
import jax.numpy as jnp
import numpy as np
from jax import lax
from jax.experimental import pallas as pl
from jax.experimental.pallas import tpu as pltpu

F32 = jnp.float32
BF16 = jnp.bfloat16

LANE = 128
SUBLANE = 8
V7X_VMEM_BYTES = 64 * 1024 * 1024

D_MODEL = 1024
MLA_HEADS = 8
QK_NOPE_DIM = 64
QK_ROPE_DIM = 32
V_HEAD_DIM = 64
Q_LORA_RANK = 256
KV_LORA_RANK = 128
ROPE_THETA = 10000.0
MLSTM_HEADS = 4
MLSTM_QK_DIM = 64
MLSTM_V_DIM = 128
CONV_K = 4
D_FF = 2816
EPS = 1e-6

MLA_WIDTH = MLA_HEADS * V_HEAD_DIM
MLSTM_WIDTH = MLSTM_HEADS * MLSTM_V_DIM
MLSTM_QK_WIDTH = MLSTM_HEADS * MLSTM_QK_DIM
Q_HEAD = QK_NOPE_DIM + QK_ROPE_DIM
KV_HEAD = QK_NOPE_DIM + V_HEAD_DIM
IN_SPLITS = (Q_LORA_RANK, KV_LORA_RANK, QK_ROPE_DIM, MLSTM_QK_WIDTH, MLSTM_QK_WIDTH,
             MLSTM_WIDTH, MLSTM_WIDTH, MLSTM_HEADS, MLSTM_HEADS)

HEAD_PAD = LANE
ATT_PAD = MLA_HEADS * HEAD_PAD
MQK_PAD = MLSTM_HEADS * HEAD_PAD
MIX_PAD = ATT_PAD + MLSTM_WIDTH

Z_CQ = 0
Z_CKV = Z_CQ + Q_LORA_RANK
Z_KR = Z_CKV + KV_LORA_RANK
Z_KRR = Z_KR + HEAD_PAD
Z_QKM = Z_KRR + HEAD_PAD
Z_VM = Z_QKM + 2 * MQK_PAD
Z_OM = Z_VM + MLSTM_WIDTH
Z_GATE = Z_OM + MLSTM_WIDTH
Z_WIDTH = Z_GATE + LANE

FFN_ROWS = 512
FFN_CHUNKS = ((0, 1536), (1536, 2816))
PROJ_ROWS = 512
ATT_BLOCK = 512
MLSTM_ROWS = 512
MLSTM_CHUNK = 128
ROPE_ROWS = 2048


def _vmem_limit(nbytes):
    return int(min(V7X_VMEM_BYTES - (4 << 20), max(nbytes, 32 << 20)))


def _rms(x, g):
    ms = jnp.mean(x * x, axis=-1, keepdims=True)
    return x * lax.rsqrt(ms + EPS) * g


def _sigmoid(x):
    return 1.0 / (1.0 + jnp.exp(-x))


def _dot(a, b):
    return jnp.dot(a, b, preferred_element_type=F32)


def _rope_body(pos_ref, inv_ref, cos_ref, sin_ref):
    ang = pos_ref[...].astype(F32) * inv_ref[...]
    lane = lax.broadcasted_iota(jnp.int32, ang.shape, 1)
    rope = (lane >= QK_NOPE_DIM) & (lane < Q_HEAD)
    cos_ref[...] = jnp.where(lane < QK_NOPE_DIM, 1.0, jnp.where(rope, jnp.cos(ang), 0.0))
    sin_ref[...] = jnp.where(rope, jnp.sin(ang), 0.0)


def _rope_tables(positions):
    n = positions.size
    rows = min(ROPE_ROWS, n)
    inv = ROPE_THETA ** (-jnp.arange(0, QK_ROPE_DIM, 2, dtype=F32) / QK_ROPE_DIM)
    inv_row = jnp.zeros((1, LANE), F32)
    inv_row = inv_row.at[0, QK_NOPE_DIM:Q_HEAD].set(jnp.concatenate([inv, inv]))
    spec = pl.BlockSpec((rows, LANE), lambda i: (i, 0))
    return pl.pallas_call(
        _rope_body,
        grid=(n // rows,),
        in_specs=[pl.BlockSpec((rows, 1), lambda i: (i, 0)),
                  pl.BlockSpec((1, LANE), lambda i: (0, 0))],
        out_specs=[spec, spec],
        out_shape=[jax.ShapeDtypeStruct((n, LANE), F32)] * 2,
        name="rope_tables",
    )(positions.reshape(n, 1), inv_row)


def _ffn_body(*refs, mix, final):
    refs = list(refs)
    o_ref = refs.pop()
    x = refs.pop(0)[...]
    if mix:
        ya_ref, ym_ref, wo_ref = refs[:3]
        refs = refs[3:]
        y = jnp.concatenate([ya_ref[...], ym_ref[...]], axis=-1)
        x = x + _dot(y, wo_ref[...])
    g_ref, wg_ref, wu_ref, wd_ref = refs[:4]
    h = _rms(x, g_ref[...]).astype(BF16)
    y = None
    for c0, c1 in FFN_CHUNKS:
        gate = _dot(h, wg_ref[:, c0:c1])
        up = _dot(h, wu_ref[:, c0:c1])
        act = (gate * _sigmoid(gate) * up).astype(BF16)
        part = _dot(act, wd_ref[c0:c1, :])
        y = part if y is None else y + part
    out = x + 0.5 * y
    if final:
        out = _rms(out, refs[4][...])
    o_ref[...] = out


def _ffn(x, layer, norm, wg, wu, wd, mix=None, final_norm=None):
    t = x.shape[0]
    rows = min(FFN_ROWS, t)

    def row_spec(w):
        return pl.BlockSpec((rows, w), lambda i: (i, 0))

    def const(shape):
        return pl.BlockSpec((None,) + shape, lambda i: (layer, 0, 0), pipeline_mode=pl.Buffered(1))

    args, specs = [x], [row_spec(D_MODEL)]
    nbytes = 4 * rows * D_MODEL * 4 + 3 * D_MODEL * D_FF * 2
    if mix is not None:
        args += list(mix)
        specs += [row_spec(ATT_PAD), row_spec(MLSTM_WIDTH), const((MIX_PAD, D_MODEL))]
        nbytes += MIX_PAD * (D_MODEL * 2 + rows * 2 * 3)
    args += [norm, wg, wu, wd]
    specs += [const((1, D_MODEL)), const((D_MODEL, D_FF)), const((D_MODEL, D_FF)),
              const((D_FF, D_MODEL))]
    if final_norm is not None:
        args.append(final_norm)
        specs.append(pl.BlockSpec((1, D_MODEL), lambda i: (0, 0)))
    widest = max(c1 - c0 for c0, c1 in FFN_CHUNKS)
    nbytes += rows * widest * (4 + 4 + 4 + 2) + rows * D_MODEL * 4 * 3
    return pl.pallas_call(
        functools.partial(_ffn_body, mix=mix is not None, final=final_norm is not None),
        grid=(t // rows,),
        in_specs=specs,
        out_specs=row_spec(D_MODEL),
        out_shape=jax.ShapeDtypeStruct((t, D_MODEL), F32),
        compiler_params=pltpu.CompilerParams(
            dimension_semantics=("parallel",), vmem_limit_bytes=_vmem_limit(nbytes + (8 << 20))),
        name="mix_ffn" if mix is not None else "ffn",
    )(*args)


def _log_sigmoid(x):
    return jnp.minimum(x, 0.0) - jnp.log1p(jnp.exp(-jnp.abs(x)))


def _proj_body(x_ref, g_ref, win_ref, qn_ref, wuq_ref, wuqr_ref, kvn_ref, wuk_ref, wuv_ref,
               cos_ref, sin_ref, gb_ref,
               q_ref, k_ref, v_ref, qkm_ref, vm_ref, om_ref, gc_ref):
    h = _rms(x_ref[...], g_ref[...]).astype(BF16)
    z = _dot(h, win_ref[...])
    cos = cos_ref[...]
    sin = sin_ref[...]
    cqn = _rms(z[:, Z_CQ:Z_CKV], qn_ref[...]).astype(BF16)
    ckvn = _rms(z[:, Z_CKV:Z_KR], kvn_ref[...]).astype(BF16)
    qa = _dot(cqn, wuq_ref[...])
    qb = _dot(cqn, wuqr_ref[...])
    kn = _dot(ckvn, wuk_ref[...])
    k_rope = z[:, Z_KR:Z_KRR] * cos + z[:, Z_KRR:Z_QKM] * sin
    scale = Q_HEAD ** -0.5
    for hd in range(MLA_HEADS):
        sl = slice(hd * HEAD_PAD, (hd + 1) * HEAD_PAD)
        q_ref[:, sl] = ((qa[:, sl] * cos + qb[:, sl] * sin) * scale).astype(BF16)
        k_ref[:, sl] = (kn[:, sl] + k_rope).astype(BF16)
    v_ref[...] = _dot(ckvn, wuv_ref[...]).astype(BF16)
    qkm_ref[...] = z[:, Z_QKM:Z_VM]
    vm_ref[...] = z[:, Z_VM:Z_OM].astype(BF16)
    om_ref[...] = z[:, Z_OM:Z_GATE]
    gz = z[:, Z_GATE:Z_WIDTH] + gb_ref[...]
    lane = lax.broadcasted_iota(jnp.int32, gz.shape, 1)
    gc_ref[...] = jnp.where(lane < MLSTM_HEADS, gz, _log_sigmoid(gz))


def _proj(x, layer, w, cos, sin):
    t = x.shape[0]
    rows = min(PROJ_ROWS, t)

    def row_spec(width):
        return pl.BlockSpec((rows, width), lambda i: (i, 0))

    def const(shape):
        return pl.BlockSpec((None,) + shape, lambda i: (layer, 0, 0), pipeline_mode=pl.Buffered(1))

    out_widths = ((ATT_PAD, BF16), (ATT_PAD, BF16), (ATT_PAD, BF16), (2 * MQK_PAD, F32),
                  (MLSTM_WIDTH, BF16), (MLSTM_WIDTH, F32), (LANE, F32))
    nbytes = (D_MODEL * Z_WIDTH * 2 + 2 * Q_LORA_RANK * ATT_PAD * 2 + 2 * KV_LORA_RANK * ATT_PAD * 2
              + rows * (Z_WIDTH * 4 + 4 * ATT_PAD * 4 + 2 * D_MODEL * 4)
              + 2 * rows * sum(wd * jnp.dtype(dt).itemsize for wd, dt in out_widths))
    return pl.pallas_call(
        _proj_body,
        grid=(t // rows,),
        in_specs=[row_spec(D_MODEL), const((1, D_MODEL)), const((D_MODEL, Z_WIDTH)),
                  const((1, Q_LORA_RANK)), const((Q_LORA_RANK, ATT_PAD)), const((Q_LORA_RANK, ATT_PAD)),
                  const((1, KV_LORA_RANK)), const((KV_LORA_RANK, ATT_PAD)), const((KV_LORA_RANK, ATT_PAD)),
                  row_spec(LANE), row_spec(LANE), const((1, LANE))],
        out_specs=[row_spec(wd) for wd, _ in out_widths],
        out_shape=[jax.ShapeDtypeStruct((t, wd), dt) for wd, dt in out_widths],
        compiler_params=pltpu.CompilerParams(
            dimension_semantics=("parallel",), vmem_limit_bytes=_vmem_limit(nbytes + (8 << 20))),
        name="in_proj",
    )(x, w["mix_norm"], w["w_in"], w["q_norm"], w["w_uq"], w["w_uq_rot"], w["kv_norm"],
      w["w_uk"], w["w_uv"], cos, sin, w["gate_bias"])


def _attn_body(q_ref, k_ref, v_ref, g_ref, o_ref, *, blk):
    qi = pl.program_id(2)
    q = q_ref[0]

    def step(j, carry, masked):
        m, l, acc = carry
        off = pl.multiple_of(j * blk, blk)
        k = k_ref[0, pl.ds(off, blk), :]
        v = v_ref[0, pl.ds(off, blk), :]
        s = lax.dot_general(q, k, (((1,), (1,)), ((), ())), preferred_element_type=F32)
        if masked:
            row = lax.broadcasted_iota(jnp.int32, s.shape, 0)
            col = lax.broadcasted_iota(jnp.int32, s.shape, 1)
            s = jnp.where(col <= row, s, -jnp.inf)
        m_new = jnp.maximum(m, jnp.max(s, axis=-1, keepdims=True))
        alpha = jnp.exp(m - m_new)
        p = jnp.exp(s - m_new)
        l = alpha * l + jnp.sum(p, axis=-1, keepdims=True)
        acc = alpha * acc + _dot(p.astype(BF16), v)
        return m_new, l, acc

    init = (jnp.full((blk, 1), -jnp.inf, F32), jnp.zeros((blk, 1), F32),
            jnp.zeros((blk, HEAD_PAD), F32))
    carry = lax.fori_loop(0, qi, lambda j, c: step(j, c, False), init)
    _, l, acc = step(qi, carry, True)
    o = acc / l
    ms = jnp.sum(o * o, axis=-1, keepdims=True) * (1.0 / V_HEAD_DIM)
    o_ref[0] = (o * lax.rsqrt(ms + EPS) * g_ref[0]).astype(BF16)


def _attention(q, k, v, g, layer):
    b, s, _ = q.shape
    blk = min(ATT_BLOCK, s)
    q_spec = pl.BlockSpec((1, blk, HEAD_PAD), lambda bi, hi, qi: (bi, qi, hi))
    kv_spec = pl.BlockSpec((1, s, HEAD_PAD), lambda bi, hi, qi: (bi, 0, hi))
    return pl.pallas_call(
        functools.partial(_attn_body, blk=blk),
        grid=(b, MLA_HEADS, s // blk),
        in_specs=[q_spec, kv_spec, kv_spec,
                  pl.BlockSpec((1, 1, HEAD_PAD), lambda bi, hi, qi: (layer * MLA_HEADS + hi, 0, 0))],
        out_specs=q_spec,
        out_shape=jax.ShapeDtypeStruct((b, s, ATT_PAD), BF16),
        compiler_params=pltpu.CompilerParams(
            dimension_semantics=("parallel", "parallel", "arbitrary"),
            vmem_limit_bytes=_vmem_limit(4 * s * HEAD_PAD * 2 + 8 * blk * blk * 4 + (8 << 20))),
        name="mla_attention",
    )(q, k, v, g)


def _chunk_cumsum(x, chunk):
    row = lax.broadcasted_iota(jnp.int32, x.shape, 0) % chunk
    d = 1
    while d < chunk:
        x = x + jnp.where(row >= d, pltpu.roll(x, d, 0), 0.0)
        d *= 2
    return x


def _mlstm_body(qkm_ref, vm_ref, om_ref, gc_ref, cw_ref, cb_ref, hn_ref, y_ref,
                ubuf, cst, mst, *, rows, chunk):
    @pl.when(pl.program_id(1) == 0)
    def _():
        ubuf[0:SUBLANE, :] = jnp.zeros((SUBLANE, 2 * MQK_PAD), F32)
        cst[...] = jnp.zeros(cst.shape, F32)
        mst[...] = jnp.zeros(mst.shape, F32)

    ubuf[SUBLANE:SUBLANE + rows, :] = qkm_ref[0]
    u = cb_ref[...]
    for j in range(CONV_K):
        start = SUBLANE - (CONV_K - 1) + j
        u = u + ubuf[start:start + rows, :] * cw_ref[j:j + 1, :]
    ubuf[0:SUBLANE, :] = ubuf[rows:rows + SUBLANE, :]
    qk = u * _sigmoid(u)

    gc = gc_ref[0]
    bcum = _chunk_cumsum(gc, chunk)
    gct = gc.T
    bct = bcum.T
    tri = (lax.broadcasted_iota(jnp.int32, (chunk, chunk), 1)
           <= lax.broadcasted_iota(jnp.int32, (chunk, chunk), 0))
    ones_col = (lax.broadcasted_iota(jnp.int32, (chunk, LANE), 1) == 0).astype(BF16)
    qscale = MLSTM_QK_DIM ** -0.5

    for hd in range(MLSTM_HEADS):
        hs = slice(hd * HEAD_PAD, (hd + 1) * HEAD_PAD)
        q_h = (qk[:, hs] * qscale).astype(BF16)
        kt_h = qk[:, MQK_PAD + hd * HEAD_PAD:MQK_PAD + (hd + 1) * HEAD_PAD].T
        c_aug = cst[hd]
        m_prev = mst[hd:hd + 1, 0:1]
        for c in range(rows // chunk):
            r = slice(c * chunk, (c + 1) * chunk)
            b_col = bcum[r, MLSTM_HEADS + hd:MLSTM_HEADS + hd + 1]
            b_row = bct[MLSTM_HEADS + hd:MLSTM_HEADS + hd + 1, r]
            i_row = gct[hd:hd + 1, r]
            b_end = b_row[:, chunk - 1:chunk]

            w_row = b_end - b_row + i_row
            g = jnp.max(w_row, axis=-1, keepdims=True)
            e_end = jnp.exp(w_row - g)

            log_d = jnp.where(tri, b_col - b_row + i_row, -jnp.inf)
            inter = b_col + m_prev
            m_t = jnp.maximum(inter, jnp.max(log_d, axis=-1, keepdims=True))
            qc = q_h[r]
            ktc = kt_h[:, r]
            p = jnp.exp(log_d - m_t) * _dot(qc, ktc.astype(BF16))
            e_inter = jnp.exp(inter - m_t)
            v_aug = jnp.concatenate([vm_ref[0, r, hd * MLSTM_V_DIM:(hd + 1) * MLSTM_V_DIM], ones_col],
                                    axis=-1)
            tot = e_inter * _dot(qc, c_aug.astype(BF16)) + _dot(p.astype(BF16), v_aug)
            den = tot[:, MLSTM_V_DIM:MLSTM_V_DIM + 1]
            hout = tot[:, :MLSTM_V_DIM] / jnp.maximum(jnp.abs(den), jnp.exp(-m_t))

            hn = _rms(hout, hn_ref[:, hd * MLSTM_V_DIM:(hd + 1) * MLSTM_V_DIM])
            gate = _sigmoid(om_ref[0, r, hd * MLSTM_V_DIM:(hd + 1) * MLSTM_V_DIM])
            y_ref[0, r, hd * MLSTM_V_DIM:(hd + 1) * MLSTM_V_DIM] = (gate * hn).astype(BF16)

            kv = _dot((ktc * e_end).astype(BF16), v_aug)
            m_new = jnp.maximum(b_end + m_prev, g)
            c_aug = jnp.exp(b_end + m_prev - m_new) * c_aug + jnp.exp(g - m_new) * kv
            m_prev = m_new
        cst[hd] = c_aug
        mst[hd:hd + 1, :] = jnp.broadcast_to(m_prev, (1, LANE))


def _mlstm(qkm, vm, om, gc, w, layer):
    b, s, _ = qkm.shape
    rows = min(MLSTM_ROWS, s)
    chunk = min(MLSTM_CHUNK, rows)

    def seq_spec(width):
        return pl.BlockSpec((1, rows, width), lambda bi, ti: (bi, ti, 0))

    def const(shape):
        return pl.BlockSpec((None,) + shape, lambda bi, ti: (layer, 0, 0))

    return pl.pallas_call(
        functools.partial(_mlstm_body, rows=rows, chunk=chunk),
        grid=(b, s // rows),
        in_specs=[seq_spec(2 * MQK_PAD), seq_spec(MLSTM_WIDTH), seq_spec(MLSTM_WIDTH), seq_spec(LANE),
                  const((CONV_K, 2 * MQK_PAD)), const((1, 2 * MQK_PAD)), const((1, MLSTM_WIDTH))],
        out_specs=seq_spec(MLSTM_WIDTH),
        out_shape=jax.ShapeDtypeStruct((b, s, MLSTM_WIDTH), BF16),
        scratch_shapes=[pltpu.VMEM((rows + SUBLANE, 2 * MQK_PAD), F32),
                        pltpu.VMEM((MLSTM_HEADS, HEAD_PAD, 2 * LANE), F32),
                        pltpu.VMEM((SUBLANE, LANE), F32)],
        compiler_params=pltpu.CompilerParams(
            dimension_semantics=("parallel", "arbitrary"),
            vmem_limit_bytes=_vmem_limit(rows * 2 * MQK_PAD * 4 * 8 + (8 << 20))),
        name="mlstm",
    )(qkm, vm, om, gc, w["conv_w"], w["conv_b"], w["mlstm_norm"])


def _gather_cols(w, src, coef):
    return jnp.take(w, jnp.asarray(src, jnp.int32), axis=-1) * jnp.asarray(coef, w.dtype)


def _pad_heads(n_heads, width, base=0, stride=None):
    stride = width if stride is None else stride
    src = np.zeros(n_heads * HEAD_PAD, np.int64)
    coef = np.zeros(n_heads * HEAD_PAD, np.float32)
    for hd in range(n_heads):
        src[hd * HEAD_PAD:hd * HEAD_PAD + width] = base + hd * stride + np.arange(width)
        coef[hd * HEAD_PAD:hd * HEAD_PAD + width] = 1.0
    return src, coef


def _rot_half_cols(base):
    half = QK_ROPE_DIM // 2
    src = np.concatenate([base + half + np.arange(half), base + np.arange(half)])
    coef = np.concatenate([-np.ones(half, np.float32), np.ones(half, np.float32)])
    return src, coef


def _prepare_weights(p):
    off = np.concatenate([[0], np.cumsum(IN_SPLITS)])
    o_cq, o_ckv, o_kr, o_qm, o_km, o_vm, o_om, o_im, o_fm = (int(v) for v in off[:-1])

    src = np.zeros(Z_WIDTH, np.int64)
    coef = np.zeros(Z_WIDTH, np.float32)

    def put(dst, s, c):
        src[dst:dst + len(s)] = s
        coef[dst:dst + len(s)] = c

    put(Z_CQ, o_cq + np.arange(Q_LORA_RANK), 1.0)
    put(Z_CKV, o_ckv + np.arange(KV_LORA_RANK), 1.0)
    put(Z_KR + QK_NOPE_DIM, o_kr + np.arange(QK_ROPE_DIM), 1.0)
    put(Z_KRR + QK_NOPE_DIM, *_rot_half_cols(o_kr))
    put(Z_QKM, *_pad_heads(MLSTM_HEADS, MLSTM_QK_DIM, o_qm))
    put(Z_QKM + MQK_PAD, *_pad_heads(MLSTM_HEADS, MLSTM_QK_DIM, o_km))
    put(Z_VM, o_vm + np.arange(MLSTM_WIDTH), 1.0)
    put(Z_OM, o_om + np.arange(MLSTM_WIDTH), 1.0)
    put(Z_GATE, o_im + np.arange(MLSTM_HEADS), 1.0)
    put(Z_GATE + MLSTM_HEADS, o_fm + np.arange(MLSTM_HEADS), 1.0)
    w_in = _gather_cols(p["w_in"], src, coef).astype(BF16)

    q_src, q_coef = _pad_heads(MLA_HEADS, Q_HEAD)
    r_src = np.zeros(ATT_PAD, np.int64)
    r_coef = np.zeros(ATT_PAD, np.float32)
    for hd in range(MLA_HEADS):
        s, c = _rot_half_cols(hd * Q_HEAD + QK_NOPE_DIM)
        lo = hd * HEAD_PAD + QK_NOPE_DIM
        r_src[lo:lo + QK_ROPE_DIM] = s
        r_coef[lo:lo + QK_ROPE_DIM] = c
    k_src, k_coef = _pad_heads(MLA_HEADS, QK_NOPE_DIM, 0, KV_HEAD)
    v_src, v_coef = _pad_heads(MLA_HEADS, V_HEAD_DIM, QK_NOPE_DIM, KV_HEAD)

    cq_src, cq_coef = _pad_heads(MLSTM_HEADS, MLSTM_QK_DIM)
    ck_src, ck_coef = _pad_heads(MLSTM_HEADS, MLSTM_QK_DIM, MLSTM_QK_WIDTH)
    conv_src = np.concatenate([cq_src, ck_src])
    conv_coef = np.concatenate([cq_coef, ck_coef])

    a_src, a_coef = _pad_heads(MLA_HEADS, V_HEAD_DIM)
    depth = p["w_in"].shape[0]
    w_out_att = jnp.swapaxes(_gather_cols(jnp.swapaxes(p["w_out"][:, :MLA_WIDTH, :], 1, 2),
                                          a_src, a_coef), 1, 2)
    w_out = jnp.concatenate([w_out_att, p["w_out"][:, MLA_WIDTH:, :]], axis=1).astype(BF16)

    gate_bias = jnp.concatenate(
        [p["b_igate"], p["b_fgate"], jnp.zeros((depth, LANE - 2 * MLSTM_HEADS), F32)], axis=-1)

    def row(a):
        return a[:, None, :]

    out = {
        "w_in": w_in,
        "mix_norm": row(p["mix_norm"]),
        "q_norm": row(p["q_latent_norm"]),
        "kv_norm": row(p["kv_latent_norm"]),
        "w_uq": _gather_cols(p["w_uq"], q_src, q_coef).astype(BF16),
        "w_uq_rot": _gather_cols(p["w_uq"], r_src, r_coef).astype(BF16),
        "w_uk": _gather_cols(p["w_ukv"], k_src, k_coef).astype(BF16),
        "w_uv": _gather_cols(p["w_ukv"], v_src, v_coef).astype(BF16),
        "gate_bias": row(gate_bias),
        "att_norm": _gather_cols(p["attn_head_norm"], a_src, a_coef).reshape(
            depth * MLA_HEADS, 1, HEAD_PAD),
        "conv_w": _gather_cols(p["conv_w"], conv_src, conv_coef),
        "conv_b": row(_gather_cols(p["conv_b"], conv_src, conv_coef)),
        "mlstm_norm": row(p["mlstm_head_norm"]),
        "w_out": w_out,
    }
    for name in ("ffn1", "ffn2"):
        out[name + "_norm"] = row(p[name + "_norm"])
        for mat in ("w_gate", "w_up", "w_down"):
            out[f"{name}_{mat}"] = p[f"{name}_{mat}"].astype(BF16)
    return out


def kernel(x, positions, ffn1_norm, ffn1_w_gate, ffn1_w_up, ffn1_w_down, mix_norm, w_in, q_latent_norm, w_uq, kv_latent_norm, w_ukv, conv_w, conv_b, b_igate, b_fgate, attn_head_norm, mlstm_head_norm, w_out, ffn2_norm, ffn2_w_gate, ffn2_w_up, ffn2_w_down, final_norm):
    b, s, d = x.shape
    depth = w_in.shape[0]
    w = _prepare_weights(dict(
        ffn1_norm=ffn1_norm, ffn1_w_gate=ffn1_w_gate, ffn1_w_up=ffn1_w_up, ffn1_w_down=ffn1_w_down,
        mix_norm=mix_norm, w_in=w_in, q_latent_norm=q_latent_norm, w_uq=w_uq,
        kv_latent_norm=kv_latent_norm, w_ukv=w_ukv, conv_w=conv_w, conv_b=conv_b,
        b_igate=b_igate, b_fgate=b_fgate, attn_head_norm=attn_head_norm,
        mlstm_head_norm=mlstm_head_norm, w_out=w_out, ffn2_norm=ffn2_norm,
        ffn2_w_gate=ffn2_w_gate, ffn2_w_up=ffn2_w_up, ffn2_w_down=ffn2_w_down))
    cos, sin = _rope_tables(positions)

    t = b * s
    xt = x.reshape(t, d)
    for layer in range(depth):
        xt = _ffn(xt, layer, w["ffn1_norm"], w["ffn1_w_gate"], w["ffn1_w_up"], w["ffn1_w_down"])
        q, k, v, qkm, vm, om, gc = _proj(xt, layer, w, cos, sin)
        seq = lambda a: a.reshape(b, s, a.shape[-1])
        y_att = _attention(seq(q), seq(k), seq(v), w["att_norm"], layer)
        y_mem = _mlstm(seq(qkm), seq(vm), seq(om), seq(gc), w, layer)
        xt = _ffn(xt, layer, w["ffn2_norm"], w["ffn2_w_gate"], w["ffn2_w_up"], w["ffn2_w_down"],
                  mix=(y_att.reshape(t, ATT_PAD), y_mem.reshape(t, MLSTM_WIDTH), w["w_out"]),
                  final_norm=final_norm.reshape(1, d) if layer == depth - 1 else None)
    return xt.reshape(b, s, d)
```

```python
import functools

import jax
import jax.numpy as jnp
import numpy as np
from jax import lax
from jax.experimental import pallas as pl
from jax.experimental.pallas import tpu as pltpu

F32 = jnp.float32
BF16 = jnp.bfloat16

LANE = 128
SUBLANE = 8
V7X_VMEM_BYTES = 64 * 1024 * 1024

D_MODEL = 1024
MLA_HEADS = 8
QK_NOPE_DIM = 64
QK_ROPE_DIM = 32
V_HEAD_DIM = 64
Q_LORA_RANK = 256
KV_LORA_RANK = 128
ROPE_THETA = 10000.0
MLSTM_HEADS = 4
MLSTM_QK_DIM = 64
MLSTM_V_DIM = 128
CONV_K = 4
D_FF = 2816
EPS = 1e-6

MLA_WIDTH = MLA_HEADS * V_HEAD_DIM
MLSTM_WIDTH = MLSTM_HEADS * MLSTM_V_DIM
MLSTM_QK_WIDTH = MLSTM_HEADS * MLSTM_QK_DIM
Q_HEAD = QK_NOPE_DIM + QK_ROPE_DIM
KV_HEAD = QK_NOPE_DIM + V_HEAD_DIM
IN_SPLITS = (Q_LORA_RANK, KV_LORA_RANK, QK_ROPE_DIM, MLSTM_QK_WIDTH, MLSTM_QK_WIDTH,
             MLSTM_WIDTH, MLSTM_WIDTH, MLSTM_HEADS, MLSTM_HEADS)

HEAD_PAD = LANE
ATT_PAD = MLA_HEADS * HEAD_PAD
MQK_PAD = MLSTM_HEADS * HEAD_PAD
MIX_PAD = ATT_PAD + MLSTM_WIDTH

Z_CQ = 0
Z_CKV = Z_CQ + Q_LORA_RANK
Z_KR = Z_CKV + KV_LORA_RANK
Z_KRR = Z_KR + HEAD_PAD
Z_GATE = Z_KRR + HEAD_PAD
Z_QKM = Z_GATE + LANE
Z_VM = Z_QKM + 2 * MQK_PAD
Z_OM = Z_VM + MLSTM_WIDTH
Z_WIDTH = Z_OM + MLSTM_WIDTH

FFN_ROWS = 512
FFN_CHUNKS = ((0, 1536), (1536, 2816))
PROJ_ROWS = 512
ATT_BLOCK = 512
ATT_HEADS_PER_STEP = 4
ATT_ROW_GROUP = 128
MLSTM_ROWS = 512
MLSTM_CHUNK = 128
ROPE_ROWS = 2048


def _vmem_limit(nbytes):
    return int(min(V7X_VMEM_BYTES - (4 << 20), max(nbytes, 32 << 20)))


def _rms(x, g):
    ms = jnp.mean(x * x, axis=-1, keepdims=True)
    return x * lax.rsqrt(ms + EPS) * g


def _sigmoid(x):
    return 1.0 / (1.0 + jnp.exp(-x))


def _dot(a, b):
    return jnp.dot(a, b, preferred_element_type=F32)


def _rope_body(pos_ref, inv_ref, cos_ref, sin_ref):
    ang = pos_ref[...].astype(F32) * inv_ref[...]
    lane = lax.broadcasted_iota(jnp.int32, ang.shape, 1)
    rope = (lane >= QK_NOPE_DIM) & (lane < Q_HEAD)
    cos_ref[...] = jnp.where(lane < QK_NOPE_DIM, 1.0, jnp.where(rope, jnp.cos(ang), 0.0))
    sin_ref[...] = jnp.where(rope, jnp.sin(ang), 0.0)


def _rope_tables(positions):
    n = positions.size
    rows = min(ROPE_ROWS, n)
    inv = ROPE_THETA ** (-jnp.arange(0, QK_ROPE_DIM, 2, dtype=F32) / QK_ROPE_DIM)
    inv_row = jnp.zeros((1, LANE), F32)
    inv_row = inv_row.at[0, QK_NOPE_DIM:Q_HEAD].set(jnp.concatenate([inv, inv]))
    spec = pl.BlockSpec((rows, LANE), lambda i: (i, 0))
    return pl.pallas_call(
        _rope_body,
        grid=(n // rows,),
        in_specs=[pl.BlockSpec((rows, 1), lambda i: (i, 0)),
                  pl.BlockSpec((1, LANE), lambda i: (0, 0))],
        out_specs=[spec, spec],
        out_shape=[jax.ShapeDtypeStruct((n, LANE), F32)] * 2,
        name="rope_tables",
    )(positions.reshape(n, 1), inv_row)


def _ffn_body(*refs, mix, final):
    refs = list(refs)
    o_ref = refs.pop()
    x = refs.pop(0)[...]
    if mix:
        ya_ref, ym_ref, wo_ref = refs[:3]
        refs = refs[3:]
        y = jnp.concatenate([ya_ref[...], ym_ref[...]], axis=-1)
        x = x + _dot(y, wo_ref[...])
    g_ref, wg_ref, wu_ref, wd_ref = refs[:4]
    h = _rms(x, g_ref[...]).astype(BF16)
    y = None
    for c0, c1 in FFN_CHUNKS:
        gate = _dot(h, wg_ref[:, c0:c1])
        up = _dot(h, wu_ref[:, c0:c1])
        act = (gate * _sigmoid(gate) * up).astype(BF16)
        part = _dot(act, wd_ref[c0:c1, :])
        y = part if y is None else y + part
    out = x + 0.5 * y
    if final:
        out = _rms(out, refs[4][...])
    o_ref[...] = out


def _ffn(x, layer, norm, wg, wu, wd, mix=None, final_norm=None):
    t = x.shape[0]
    rows = min(FFN_ROWS, t)

    def row_spec(w):
        return pl.BlockSpec((rows, w), lambda i: (i, 0))

    def const(shape):
        return pl.BlockSpec((None,) + shape, lambda i: (layer, 0, 0), pipeline_mode=pl.Buffered(1))

    args, specs = [x], [row_spec(D_MODEL)]
    nbytes = 4 * rows * D_MODEL * 4 + 3 * D_MODEL * D_FF * 2
    if mix is not None:
        args += list(mix)
        specs += [row_spec(ATT_PAD), row_spec(MLSTM_WIDTH), const((MIX_PAD, D_MODEL))]
        nbytes += MIX_PAD * (D_MODEL * 2 + rows * 2 * 3)
    args += [norm, wg, wu, wd]
    specs += [const((1, D_MODEL)), const((D_MODEL, D_FF)), const((D_MODEL, D_FF)),
              const((D_FF, D_MODEL))]
    if final_norm is not None:
        args.append(final_norm)
        specs.append(pl.BlockSpec((1, D_MODEL), lambda i: (0, 0)))
    widest = max(c1 - c0 for c0, c1 in FFN_CHUNKS)
    nbytes += rows * widest * (4 + 4 + 4 + 2) + rows * D_MODEL * 4 * 3
    return pl.pallas_call(
        functools.partial(_ffn_body, mix=mix is not None, final=final_norm is not None),
        grid=(t // rows,),
        in_specs=specs,
        out_specs=row_spec(D_MODEL),
        out_shape=jax.ShapeDtypeStruct((t, D_MODEL), F32),
        compiler_params=pltpu.CompilerParams(
            dimension_semantics=("parallel",), vmem_limit_bytes=_vmem_limit(nbytes + (8 << 20))),
        name="mix_ffn" if mix is not None else "ffn",
    )(*args)


def _log_sigmoid(x):
    return jnp.minimum(x, 0.0) - jnp.log1p(jnp.exp(-jnp.abs(x)))


def _proj_body(x_ref, g_ref, win_ref, qn_ref, wuq_ref, wuqr_ref, kvn_ref, wuk_ref, wuv_ref,
               cos_ref, sin_ref, gb_ref, cw_ref, cb_ref,
               q_ref, k_ref, v_ref, qm_ref, kt_ref, vm_ref, om_ref, gc_ref, ubuf,
               *, rows, steps_per_seq):
    h = _rms(x_ref[...], g_ref[...]).astype(BF16)

    @pl.when(pl.program_id(0) % steps_per_seq == 0)
    def _():
        ubuf[0:SUBLANE, :] = jnp.zeros((SUBLANE, 2 * MQK_PAD), F32)

    ubuf[SUBLANE:SUBLANE + rows, :] = _dot(h, win_ref[:, Z_QKM:Z_VM])
    u = cb_ref[...]
    for j in range(CONV_K):
        start = SUBLANE - (CONV_K - 1) + j
        u = u + ubuf[start:start + rows, :] * cw_ref[j:j + 1, :]
    ubuf[0:SUBLANE, :] = ubuf[rows:rows + SUBLANE, :]
    qk = u * _sigmoid(u)
    qm_ref[...] = (qk[:, :MQK_PAD] * MLSTM_QK_DIM ** -0.5).astype(BF16)
    for hd in range(MLSTM_HEADS):
        sl = slice(hd * HEAD_PAD, (hd + 1) * HEAD_PAD)
        kt_ref[sl, :] = qk[:, MQK_PAD + hd * HEAD_PAD:MQK_PAD + (hd + 1) * HEAD_PAD].T

    za = _dot(h, win_ref[:, Z_CQ:Z_QKM])
    cos = cos_ref[...]
    sin = sin_ref[...]
    cqn = _rms(za[:, Z_CQ:Z_CKV], qn_ref[...]).astype(BF16)
    ckvn = _rms(za[:, Z_CKV:Z_KR], kvn_ref[...]).astype(BF16)
    qa = _dot(cqn, wuq_ref[...])
    qb = _dot(cqn, wuqr_ref[...])
    kn = _dot(ckvn, wuk_ref[...])
    k_rope = za[:, Z_KR:Z_KRR] * cos + za[:, Z_KRR:Z_GATE] * sin
    vv = _dot(ckvn, wuv_ref[...])
    scale = Q_HEAD ** -0.5 * np.log2(np.e)
    ones_lane = lax.broadcasted_iota(jnp.int32, cos.shape, 1) == V_HEAD_DIM
    for hd in range(MLA_HEADS):
        sl = slice(hd * HEAD_PAD, (hd + 1) * HEAD_PAD)
        q_ref[:, sl] = ((qa[:, sl] * cos + qb[:, sl] * sin) * scale).astype(BF16)
        k_ref[:, sl] = (kn[:, sl] + k_rope).astype(BF16)
        v_ref[:, sl] = jnp.where(ones_lane, 1.0, vv[:, sl]).astype(BF16)

    zb = _dot(h, win_ref[:, Z_VM:Z_WIDTH])
    vm_ref[...] = zb[:, :MLSTM_WIDTH].astype(BF16)
    om_ref[...] = zb[:, MLSTM_WIDTH:]
    gz = za[:, Z_GATE:Z_QKM] + gb_ref[...]
    lane = lax.broadcasted_iota(jnp.int32, gz.shape, 1)
    gc_ref[...] = jnp.where(lane < MLSTM_HEADS, gz, _log_sigmoid(gz))


def _proj(x, layer, w, cos, sin, seq_len):
    t = x.shape[0]
    rows = min(PROJ_ROWS, seq_len)

    def row_spec(width):
        return pl.BlockSpec((rows, width), lambda i: (i, 0))

    def const(shape):
        return pl.BlockSpec((None,) + shape, lambda i: (layer, 0, 0), pipeline_mode=pl.Buffered(1))

    out_widths = ((ATT_PAD, BF16), (ATT_PAD, BF16), (ATT_PAD, BF16), (MQK_PAD, BF16), None,
                  (MLSTM_WIDTH, BF16), (MLSTM_WIDTH, F32), (LANE, F32))
    kt_spec = pl.BlockSpec((MQK_PAD, rows), lambda i: (0, i))
    kt_shape = jax.ShapeDtypeStruct((MQK_PAD, t), F32)
    nbytes = (D_MODEL * Z_WIDTH * 2 + 2 * Q_LORA_RANK * ATT_PAD * 2 + 2 * KV_LORA_RANK * ATT_PAD * 2
              + rows * (Z_WIDTH * 4 + 4 * ATT_PAD * 4 + 2 * D_MODEL * 4 + 6 * MQK_PAD * 4)
              + 2 * rows * (MQK_PAD * 4 + sum(wd[0] * jnp.dtype(wd[1]).itemsize
                                              for wd in out_widths if wd is not None)))
    return pl.pallas_call(
        functools.partial(_proj_body, rows=rows, steps_per_seq=seq_len // rows),
        grid=(t // rows,),
        in_specs=[row_spec(D_MODEL), const((1, D_MODEL)), const((D_MODEL, Z_WIDTH)),
                  const((1, Q_LORA_RANK)), const((Q_LORA_RANK, ATT_PAD)), const((Q_LORA_RANK, ATT_PAD)),
                  const((1, KV_LORA_RANK)), const((KV_LORA_RANK, ATT_PAD)), const((KV_LORA_RANK, ATT_PAD)),
                  row_spec(LANE), row_spec(LANE), const((1, LANE)),
                  const((CONV_K, 2 * MQK_PAD)), const((1, 2 * MQK_PAD))],
        out_specs=[kt_spec if wd is None else row_spec(wd[0]) for wd in out_widths],
        out_shape=[kt_shape if wd is None else jax.ShapeDtypeStruct((t, wd[0]), wd[1])
                   for wd in out_widths],
        scratch_shapes=[pltpu.VMEM((rows + SUBLANE, 2 * MQK_PAD), F32)],
        compiler_params=pltpu.CompilerParams(
            dimension_semantics=("arbitrary",), vmem_limit_bytes=_vmem_limit(nbytes + (8 << 20))),
        name="in_proj",
    )(x, w["mix_norm"], w["w_in"], w["q_norm"], w["w_uq"], w["w_uq_rot"], w["kv_norm"],
      w["w_uk"], w["w_uv"], cos, sin, w["gate_bias"], w["conv_w"], w["conv_b"])


def _attn_body(q_ref, k_ref, v_ref, g_ref, o_ref, s_a, s_b, p_a, p_b, m_scr, acc_scr, *, blk, heads):
    tk = blk // 2
    qi = pl.program_id(2)
    low = slice(tk, blk)

    def produce(buf, hd, j, rows=slice(None)):
        sl = slice(hd * HEAD_PAD, (hd + 1) * HEAD_PAD)
        k = k_ref[0, pl.ds(pl.multiple_of(j * tk, tk), tk), sl]
        buf[hd, rows] = lax.dot_general(q_ref[0, rows, sl], k, (((1,), (1,)), ((), ())),
                                        preferred_element_type=F32)

    def consume(buf, p_buf, hd, j, rows=slice(0, blk), mask=None):
        sl = slice(hd * HEAD_PAD, (hd + 1) * HEAD_PAD)
        for r0 in range(rows.start, rows.stop, ATT_ROW_GROUP):
            rg = slice(r0, r0 + ATT_ROW_GROUP)
            s = buf[hd, rg]
            if mask is not None:
                s = jnp.where(mask[r0 - rows.start:r0 - rows.start + ATT_ROW_GROUP], s, -jnp.inf)
            m_old = m_scr[hd, rg]
            m_new = jnp.maximum(m_old, jnp.max(s, axis=-1, keepdims=True))
            p_buf[hd, rg] = jnp.exp2(s - jnp.concatenate([m_new] * (tk // LANE), axis=1)).astype(BF16)
            acc_scr[hd, rg] = jnp.exp2(m_old - m_new) * acc_scr[hd, rg]
            m_scr[hd, rg] = m_new
        v = v_ref[0, pl.ds(pl.multiple_of(j * tk, tk), tk), sl]
        acc_scr[hd, rows] += _dot(p_buf[hd, rows], v)

    m_scr[...] = jnp.full(m_scr.shape, -jnp.inf, F32)
    acc_scr[...] = jnp.zeros(acc_scr.shape, F32)
    for hd in range(heads):
        produce(s_a, hd, 0)

    def pair(t, carry):
        for hd in range(heads):
            produce(s_b, hd, 2 * t + 1)
        for hd in range(heads):
            consume(s_a, p_a, hd, 2 * t)
        for hd in range(heads):
            produce(s_a, hd, 2 * t + 2)
        for hd in range(heads):
            consume(s_b, p_b, hd, 2 * t + 1)
        return carry

    lax.fori_loop(0, qi, pair, 0)

    tri = (lax.broadcasted_iota(jnp.int32, (blk, tk), 1)
           <= lax.broadcasted_iota(jnp.int32, (blk, tk), 0))
    for hd in range(heads):
        produce(s_b, hd, 2 * qi + 1, rows=low)
    for hd in range(heads):
        consume(s_a, p_a, hd, 2 * qi, mask=tri)
    for hd in range(heads):
        consume(s_b, p_b, hd, 2 * qi + 1, rows=low, mask=tri[:tk])

    lane = lax.broadcasted_iota(jnp.int32, (blk, HEAD_PAD), 1)
    for hd in range(heads):
        sl = slice(hd * HEAD_PAD, (hd + 1) * HEAD_PAD)
        acc = acc_scr[hd]
        o = jnp.where(lane < V_HEAD_DIM, acc / acc[:, V_HEAD_DIM:V_HEAD_DIM + 1], 0.0)
        ms = jnp.sum(o * o, axis=-1, keepdims=True) * (1.0 / V_HEAD_DIM)
        o_ref[0, :, sl] = (o * lax.rsqrt(ms + EPS) * g_ref[:, sl]).astype(BF16)


def _attention(q, k, v, g, layer):
    b, s, _ = q.shape
    blk = min(ATT_BLOCK, s)
    width = ATT_HEADS_PER_STEP * HEAD_PAD
    q_spec = pl.BlockSpec((1, blk, width), lambda bi, hi, qi: (bi, qi, hi))
    kv_spec = pl.BlockSpec((1, s, width), lambda bi, hi, qi: (bi, 0, hi))
    return pl.pallas_call(
        functools.partial(_attn_body, blk=blk, heads=ATT_HEADS_PER_STEP),
        grid=(b, MLA_HEADS // ATT_HEADS_PER_STEP, s // blk),
        in_specs=[q_spec, kv_spec, kv_spec,
                  pl.BlockSpec((None, 1, width), lambda bi, hi, qi: (layer, 0, hi))],
        out_specs=q_spec,
        out_shape=jax.ShapeDtypeStruct((b, s, ATT_PAD), BF16),
        scratch_shapes=[pltpu.VMEM((ATT_HEADS_PER_STEP, blk, blk // 2), F32),
                        pltpu.VMEM((ATT_HEADS_PER_STEP, blk, blk // 2), F32),
                        pltpu.VMEM((ATT_HEADS_PER_STEP, blk, blk // 2), BF16),
                        pltpu.VMEM((ATT_HEADS_PER_STEP, blk, blk // 2), BF16),
                        pltpu.VMEM((ATT_HEADS_PER_STEP, blk, HEAD_PAD), F32),
                        pltpu.VMEM((ATT_HEADS_PER_STEP, blk, HEAD_PAD), F32)],
        compiler_params=pltpu.CompilerParams(
            dimension_semantics=("parallel", "parallel", "arbitrary"),
            vmem_limit_bytes=_vmem_limit(4 * s * width * 2 + ATT_HEADS_PER_STEP * 8 * blk * blk * 4
                                         + (8 << 20))),
        name="mla_attention",
    )(q, k, v, g)


def _chunk_scan(x, chunk, op, fill):
    row = lax.broadcasted_iota(jnp.int32, x.shape, 0) % chunk
    d = 1
    while d < chunk:
        x = op(x, jnp.where(row >= d, pltpu.roll(x, d, 0), fill))
        d *= 2
    return x


def _split3(x):
    hi = x.astype(BF16)
    r1 = x - hi.astype(F32)
    mid = r1.astype(BF16)
    lo = (r1 - mid.astype(F32)).astype(BF16)
    return jnp.concatenate([hi, mid, lo], axis=-1)


def _mlstm_body(qm_ref, kt_ref, vm_ref, om_ref, gc_ref, rep_ref, hn_ref, y_ref, cst, mst,
                *, rows, chunk):
    @pl.when(pl.program_id(1) == 0)
    def _():
        cst[...] = jnp.zeros(cst.shape, F32)
        mst[...] = jnp.zeros(mst.shape, F32)

    gc = gc_ref[0]
    lane = lax.broadcasted_iota(jnp.int32, gc.shape, 1)
    bcum = _chunk_scan(gc, chunk, jnp.add, 0.0)
    a_col = gc - pltpu.roll(bcum, LANE - MLSTM_HEADS, 1)
    a_max = _chunk_scan(a_col, chunk, jnp.maximum, -jnp.inf)
    a_row = a_col.T
    rep = _dot(_split3(jnp.where(lane < MLSTM_HEADS, a_max, bcum)), rep_ref[...])

    tri = (lax.broadcasted_iota(jnp.int32, (chunk, chunk), 1)
           <= lax.broadcasted_iota(jnp.int32, (chunk, chunk), 0))
    ones = jnp.ones((chunk, LANE), BF16)

    for hd in range(MLSTM_HEADS):
        hs = slice(hd * HEAD_PAD, (hd + 1) * HEAD_PAD)
        vs = slice(hd * MLSTM_V_DIM, (hd + 1) * MLSTM_V_DIM)
        c_aug = cst[hd]
        m_prev = mst[hd:hd + 1, :]
        for c in range(rows // chunk):
            r = slice(c * chunk, (c + 1) * chunk)
            amax_rep = rep[r, hs]
            b_rep = rep[r, MQK_PAD + hd * HEAD_PAD:MQK_PAD + (hd + 1) * HEAD_PAD]
            a_r = a_row[hd:hd + 1, r]
            a_top = amax_rep[chunk - 1:chunk, :]
            b_end = b_rep[chunk - 1:chunk, :]

            big_m = jnp.maximum(m_prev, amax_rep)
            qc = qm_ref[0, r, hs]
            ktc = kt_ref[hs, r]
            p = jnp.exp(jnp.where(tri, a_r - big_m, -jnp.inf)) * _dot(qc, ktc.astype(BF16))
            e_inter = jnp.exp(m_prev - big_m)
            v_aug = jnp.concatenate([vm_ref[0, r, vs], ones], axis=-1)
            qc_state = _dot(qc, c_aug.astype(BF16))
            pv = _dot(p.astype(BF16), v_aug)
            num = e_inter * qc_state[:, :MLSTM_V_DIM] + pv[:, :MLSTM_V_DIM]
            den = e_inter * qc_state[:, MLSTM_V_DIM:] + pv[:, MLSTM_V_DIM:]
            hout = num / jnp.maximum(jnp.abs(den), jnp.exp(-(b_rep + big_m)))

            hn = _rms(hout, hn_ref[:, vs])
            y_ref[0, r, vs] = (_sigmoid(om_ref[0, r, vs]) * hn).astype(BF16)

            kv = _dot((ktc * jnp.exp(a_r - a_top)).astype(BF16), v_aug)
            m_top = jnp.maximum(m_prev, a_top)
            d_old = jnp.exp(m_prev - m_top)
            d_new = jnp.exp(a_top - m_top)
            c_aug = (jnp.concatenate([d_old, d_old], axis=1) * c_aug
                     + jnp.concatenate([d_new, d_new], axis=1) * kv)
            m_prev = b_end + m_top
        cst[hd] = c_aug
        mst[hd:hd + 1, :] = m_prev


def _mlstm(qm, kt, vm, om, gc, w, layer):
    b, s, _ = qm.shape
    rows = min(MLSTM_ROWS, s)
    chunk = min(MLSTM_CHUNK, rows)
    steps = s // rows

    def seq_spec(width):
        return pl.BlockSpec((1, rows, width), lambda bi, ti: (bi, ti, 0))

    return pl.pallas_call(
        functools.partial(_mlstm_body, rows=rows, chunk=chunk),
        grid=(b, steps),
        in_specs=[seq_spec(MQK_PAD), pl.BlockSpec((MQK_PAD, rows), lambda bi, ti: (0, bi * steps + ti)),
                  seq_spec(MLSTM_WIDTH), seq_spec(MLSTM_WIDTH), seq_spec(LANE),
                  pl.BlockSpec((3 * LANE, 2 * MQK_PAD), lambda bi, ti: (0, 0)),
                  pl.BlockSpec((None, 1, MLSTM_WIDTH), lambda bi, ti: (layer, 0, 0))],
        out_specs=seq_spec(MLSTM_WIDTH),
        out_shape=jax.ShapeDtypeStruct((b, s, MLSTM_WIDTH), BF16),
        scratch_shapes=[pltpu.VMEM((MLSTM_HEADS, HEAD_PAD, 2 * LANE), F32),
                        pltpu.VMEM((SUBLANE, LANE), F32)],
        compiler_params=pltpu.CompilerParams(
            dimension_semantics=("parallel", "arbitrary"),
            vmem_limit_bytes=_vmem_limit(rows * 2 * MQK_PAD * 4 * 8 + (8 << 20))),
        name="mlstm",
    )(qm, kt, vm, om, gc, w["gate_rep"], w["mlstm_norm"])


def _gather_cols(w, src, coef):
    return jnp.take(w, jnp.asarray(src, jnp.int32), axis=-1) * jnp.asarray(coef, w.dtype)


def _pad_heads(n_heads, width, base=0, stride=None):
    stride = width if stride is None else stride
    src = np.zeros(n_heads * HEAD_PAD, np.int64)
    coef = np.zeros(n_heads * HEAD_PAD, np.float32)
    for hd in range(n_heads):
        src[hd * HEAD_PAD:hd * HEAD_PAD + width] = base + hd * stride + np.arange(width)
        coef[hd * HEAD_PAD:hd * HEAD_PAD + width] = 1.0
    return src, coef


def _rot_half_cols(base):
    half = QK_ROPE_DIM // 2
    src = np.concatenate([base + half + np.arange(half), base + np.arange(half)])
    coef = np.concatenate([-np.ones(half, np.float32), np.ones(half, np.float32)])
    return src, coef


def _prepare_weights(p):
    off = np.concatenate([[0], np.cumsum(IN_SPLITS)])
    o_cq, o_ckv, o_kr, o_qm, o_km, o_vm, o_om, o_im, o_fm = (int(v) for v in off[:-1])

    src = np.zeros(Z_WIDTH, np.int64)
    coef = np.zeros(Z_WIDTH, np.float32)

    def put(dst, s, c):
        src[dst:dst + len(s)] = s
        coef[dst:dst + len(s)] = c

    put(Z_CQ, o_cq + np.arange(Q_LORA_RANK), 1.0)
    put(Z_CKV, o_ckv + np.arange(KV_LORA_RANK), 1.0)
    put(Z_KR + QK_NOPE_DIM, o_kr + np.arange(QK_ROPE_DIM), 1.0)
    put(Z_KRR + QK_NOPE_DIM, *_rot_half_cols(o_kr))
    put(Z_QKM, *_pad_heads(MLSTM_HEADS, MLSTM_QK_DIM, o_qm))
    put(Z_QKM + MQK_PAD, *_pad_heads(MLSTM_HEADS, MLSTM_QK_DIM, o_km))
    put(Z_VM, o_vm + np.arange(MLSTM_WIDTH), 1.0)
    put(Z_OM, o_om + np.arange(MLSTM_WIDTH), 1.0)
    put(Z_GATE, o_im + np.arange(MLSTM_HEADS), 1.0)
    put(Z_GATE + MLSTM_HEADS, o_fm + np.arange(MLSTM_HEADS), 1.0)
    w_in = _gather_cols(p["w_in"], src, coef).astype(BF16)

    q_src, q_coef = _pad_heads(MLA_HEADS, Q_HEAD)
    r_src = np.zeros(ATT_PAD, np.int64)
    r_coef = np.zeros(ATT_PAD, np.float32)
    for hd in range(MLA_HEADS):
        s, c = _rot_half_cols(hd * Q_HEAD + QK_NOPE_DIM)
        lo = hd * HEAD_PAD + QK_NOPE_DIM
        r_src[lo:lo + QK_ROPE_DIM] = s
        r_coef[lo:lo + QK_ROPE_DIM] = c
    k_src, k_coef = _pad_heads(MLA_HEADS, QK_NOPE_DIM, 0, KV_HEAD)
    v_src, v_coef = _pad_heads(MLA_HEADS, V_HEAD_DIM, QK_NOPE_DIM, KV_HEAD)

    cq_src, cq_coef = _pad_heads(MLSTM_HEADS, MLSTM_QK_DIM)
    ck_src, ck_coef = _pad_heads(MLSTM_HEADS, MLSTM_QK_DIM, MLSTM_QK_WIDTH)
    conv_src = np.concatenate([cq_src, ck_src])
    conv_coef = np.concatenate([cq_coef, ck_coef])

    a_src, a_coef = _pad_heads(MLA_HEADS, V_HEAD_DIM)
    depth = p["w_in"].shape[0]
    w_out_att = jnp.swapaxes(_gather_cols(jnp.swapaxes(p["w_out"][:, :MLA_WIDTH, :], 1, 2),
                                          a_src, a_coef), 1, 2)
    w_out = jnp.concatenate([w_out_att, p["w_out"][:, MLA_WIDTH:, :]], axis=1).astype(BF16)

    gate_bias = jnp.concatenate(
        [p["b_igate"], p["b_fgate"], jnp.zeros((depth, LANE - 2 * MLSTM_HEADS), F32)], axis=-1)

    def row(a):
        return a[:, None, :]

    gate_rep = np.zeros((3, LANE, 2 * MQK_PAD), np.float32)
    for hd in range(MLSTM_HEADS):
        gate_rep[:, hd, hd * HEAD_PAD:(hd + 1) * HEAD_PAD] = 1.0
        gate_rep[:, MLSTM_HEADS + hd, MQK_PAD + hd * HEAD_PAD:MQK_PAD + (hd + 1) * HEAD_PAD] = 1.0

    out = {
        "gate_rep": jnp.asarray(gate_rep.reshape(3 * LANE, 2 * MQK_PAD), BF16),
        "w_in": w_in,
        "mix_norm": row(p["mix_norm"]),
        "q_norm": row(p["q_latent_norm"]),
        "kv_norm": row(p["kv_latent_norm"]),
        "w_uq": _gather_cols(p["w_uq"], q_src, q_coef).astype(BF16),
        "w_uq_rot": _gather_cols(p["w_uq"], r_src, r_coef).astype(BF16),
        "w_uk": _gather_cols(p["w_ukv"], k_src, k_coef).astype(BF16),
        "w_uv": _gather_cols(p["w_ukv"], v_src, v_coef).astype(BF16),
        "gate_bias": row(gate_bias),
        "att_norm": row(_gather_cols(p["attn_head_norm"], a_src, a_coef)),
        "conv_w": _gather_cols(p["conv_w"], conv_src, conv_coef),
        "conv_b": row(_gather_cols(p["conv_b"], conv_src, conv_coef)),
        "mlstm_norm": row(p["mlstm_head_norm"]),
        "w_out": w_out,
    }
    for name in ("ffn1", "ffn2"):
        out[name + "_norm"] = row(p[name + "_norm"])
        for mat in ("w_gate", "w_up", "w_down"):
            out[f"{name}_{mat}"] = p[f"{name}_{mat}"].astype(BF16)
    return out


def kernel(x, positions, ffn1_norm, ffn1_w_gate, ffn1_w_up, ffn1_w_down, mix_norm, w_in, q_latent_norm, w_uq, kv_latent_norm, w_ukv, conv_w, conv_b, b_igate, b_fgate, attn_head_norm, mlstm_head_norm, w_out, ffn2_norm, ffn2_w_gate, ffn2_w_up, ffn2_w_down, final_norm):
    b, s, d = x.shape
    depth = w_in.shape[0]
    w = _prepare_weights(dict(
        ffn1_norm=ffn1_norm, ffn1_w_gate=ffn1_w_gate, ffn1_w_up=ffn1_w_up, ffn1_w_down=ffn1_w_down,
        mix_norm=mix_norm, w_in=w_in, q_latent_norm=q_latent_norm, w_uq=w_uq,
        kv_latent_norm=kv_latent_norm, w_ukv=w_ukv, conv_w=conv_w, conv_b=conv_b,
        b_igate=b_igate, b_fgate=b_fgate, attn_head_norm=attn_head_norm,
        mlstm_head_norm=mlstm_head_norm, w_out=w_out, ffn2_norm=ffn2_norm,
        ffn2_w_gate=ffn2_w_gate, ffn2_w_up=ffn2_w_up, ffn2_w_down=ffn2_w_down))
    cos, sin = _rope_tables(positions)

    t = b * s
    xt = x.reshape(t, d)
    for layer in range(depth):
        xt = _ffn(xt, layer, w["ffn1_norm"], w["ffn1_w_gate"], w["ffn1_w_up"], w["ffn1_w_down"])
        q, k, v, qm, kt, vm, om, gc = _proj(xt, layer, w, cos, sin, s)
        seq = lambda a: a.reshape(b, s, a.shape[-1])
        y_att = _attention(seq(q), seq(k), seq(v), w["att_norm"], layer)
        y_mem = _mlstm(seq(qm), kt, seq(vm), seq(om), seq(gc), w, layer)
        xt = _ffn(xt, layer, w["ffn2_norm"], w["ffn2_w_gate"], w["ffn2_w_up"], w["ffn2_w_down"],
                  mix=(y_att.reshape(t, ATT_PAD), y_mem.reshape(t, MLSTM_WIDTH), w["w_out"]),
                  final_norm=final_norm.reshape(1, d) if layer == depth - 1 else None)
    return xt.reshape(b, s, d)
```

```python
import functools

import jax
import jax.numpy as jnp
import numpy as np
from jax import lax
from jax.experimental import pallas as pl
from jax.experimental.pallas import tpu as pltpu

F32 = jnp.float32
BF16 = jnp.bfloat16

LANE = 128
SUBLANE = 8
V7X_VMEM_BYTES = 64 * 1024 * 1024

D_MODEL = 1024
MLA_HEADS = 8
QK_NOPE_DIM = 64
QK_ROPE_DIM = 32
V_HEAD_DIM = 64
Q_LORA_RANK = 256
KV_LORA_RANK = 128
ROPE_THETA = 10000.0
MLSTM_HEADS = 4
MLSTM_QK_DIM = 64
MLSTM_V_DIM = 128
CONV_K = 4
D_FF = 2816
EPS = 1e-6

MLA_WIDTH = MLA_HEADS * V_HEAD_DIM
MLSTM_WIDTH = MLSTM_HEADS * MLSTM_V_DIM
MLSTM_QK_WIDTH = MLSTM_HEADS * MLSTM_QK_DIM
Q_HEAD = QK_NOPE_DIM + QK_ROPE_DIM
KV_HEAD = QK_NOPE_DIM + V_HEAD_DIM
IN_SPLITS = (Q_LORA_RANK, KV_LORA_RANK, QK_ROPE_DIM, MLSTM_QK_WIDTH, MLSTM_QK_WIDTH,
             MLSTM_WIDTH, MLSTM_WIDTH, MLSTM_HEADS, MLSTM_HEADS)

HEAD_PAD = LANE
ATT_PAD = MLA_HEADS * HEAD_PAD
MQK_PAD = MLSTM_HEADS * HEAD_PAD
MIX_PAD = ATT_PAD + MLSTM_WIDTH
assert 2 * MLSTM_QK_DIM == HEAD_PAD and 2 * V_HEAD_DIM == HEAD_PAD

Z_CQ = 0
Z_CKV = Z_CQ + Q_LORA_RANK
Z_KR = Z_CKV + KV_LORA_RANK
Z_KRR = Z_KR + HEAD_PAD
Z_GATE = Z_KRR + HEAD_PAD
Z_QKM = Z_GATE + LANE
Z_VM = Z_QKM + MQK_PAD
Z_OM = Z_VM + MLSTM_WIDTH
Z_WIDTH = Z_OM + MLSTM_WIDTH

FFN_ROWS = 512
FFN_CHUNKS = ((0, 1536), (1536, 2816))
PROJ_ROWS = 512
ATT_BLOCK = 512
ATT_HEADS_PER_STEP = 4
ATT_ROW_GROUP = 128
MLSTM_ROWS = 512
MLSTM_CHUNK = 128
ROPE_ROWS = 2048


def _vmem_limit(nbytes):
    return int(min(V7X_VMEM_BYTES - (4 << 20), max(nbytes, 32 << 20)))


def _rms(x, g):
    ms = jnp.mean(x * x, axis=-1, keepdims=True)
    return x * lax.rsqrt(ms + EPS) * g


def _sigmoid(x):
    return 1.0 / (1.0 + jnp.exp(-x))


def _dot(a, b):
    return jnp.dot(a, b, preferred_element_type=F32)


def _rope_body(pos_ref, inv_ref, cos_ref, sin_ref):
    ang = pos_ref[...].astype(F32) * inv_ref[...]
    lane = lax.broadcasted_iota(jnp.int32, ang.shape, 1)
    rope = (lane >= QK_NOPE_DIM) & (lane < Q_HEAD)
    cos_ref[...] = jnp.where(lane < QK_NOPE_DIM, 1.0, jnp.where(rope, jnp.cos(ang), 0.0))
    sin_ref[...] = jnp.where(rope, jnp.sin(ang), 0.0)


def _rope_tables(positions):
    n = positions.size
    rows = min(ROPE_ROWS, n)
    inv = ROPE_THETA ** (-jnp.arange(0, QK_ROPE_DIM, 2, dtype=F32) / QK_ROPE_DIM)
    inv_row = jnp.zeros((1, LANE), F32)
    inv_row = inv_row.at[0, QK_NOPE_DIM:Q_HEAD].set(jnp.concatenate([inv, inv]))
    spec = pl.BlockSpec((rows, LANE), lambda i: (i, 0))
    return pl.pallas_call(
        _rope_body,
        grid=(n // rows,),
        in_specs=[pl.BlockSpec((rows, 1), lambda i: (i, 0)),
                  pl.BlockSpec((1, LANE), lambda i: (0, 0))],
        out_specs=[spec, spec],
        out_shape=[jax.ShapeDtypeStruct((n, LANE), F32)] * 2,
        name="rope_tables",
    )(positions.reshape(n, 1), inv_row)


def _ffn_body(*refs, mix, final):
    refs = list(refs)
    o_ref = refs.pop()
    x = refs.pop(0)[...]
    if mix:
        ya_ref, ym_ref, wo_ref = refs[:3]
        refs = refs[3:]
        y = jnp.concatenate([ya_ref[...], ym_ref[...]], axis=-1)
        x = x + _dot(y, wo_ref[...])
    g_ref, wg_ref, wu_ref, wd_ref = refs[:4]
    h = _rms(x, g_ref[...]).astype(BF16)
    y = None
    for c0, c1 in FFN_CHUNKS:
        gate = _dot(h, wg_ref[:, c0:c1])
        up = _dot(h, wu_ref[:, c0:c1])
        act = (gate * _sigmoid(gate) * up).astype(BF16)
        part = _dot(act, wd_ref[c0:c1, :])
        y = part if y is None else y + part
    out = x + 0.5 * y
    if final:
        out = _rms(out, refs[4][...])
    o_ref[...] = out


def _ffn(x, layer, norm, wg, wu, wd, mix=None, final_norm=None):
    t = x.shape[0]
    rows = min(FFN_ROWS, t)

    def row_spec(w):
        return pl.BlockSpec((rows, w), lambda i: (i, 0))

    def const(shape):
        return pl.BlockSpec((None,) + shape, lambda i: (layer, 0, 0), pipeline_mode=pl.Buffered(1))

    args, specs = [x], [row_spec(D_MODEL)]
    nbytes = 4 * rows * D_MODEL * 4 + 3 * D_MODEL * D_FF * 2
    if mix is not None:
        args += list(mix)
        specs += [row_spec(ATT_PAD), row_spec(MLSTM_WIDTH), const((MIX_PAD, D_MODEL))]
        nbytes += MIX_PAD * (D_MODEL * 2 + rows * 2 * 3)
    args += [norm, wg, wu, wd]
    specs += [const((1, D_MODEL)), const((D_MODEL, D_FF)), const((D_MODEL, D_FF)),
              const((D_FF, D_MODEL))]
    if final_norm is not None:
        args.append(final_norm)
        specs.append(pl.BlockSpec((1, D_MODEL), lambda i: (0, 0)))
    widest = max(c1 - c0 for c0, c1 in FFN_CHUNKS)
    nbytes += rows * widest * (4 + 4 + 4 + 2) + rows * D_MODEL * 4 * 3
    return pl.pallas_call(
        functools.partial(_ffn_body, mix=mix is not None, final=final_norm is not None),
        grid=(t // rows,),
        in_specs=specs,
        out_specs=row_spec(D_MODEL),
        out_shape=jax.ShapeDtypeStruct((t, D_MODEL), F32),
        compiler_params=pltpu.CompilerParams(
            dimension_semantics=("parallel",), vmem_limit_bytes=_vmem_limit(nbytes + (8 << 20))),
        name="mix_ffn" if mix is not None else "ffn",
    )(*args)


def _log_sigmoid(x):
    return jnp.minimum(x, 0.0) - jnp.log1p(jnp.exp(-jnp.abs(x)))


def _proj_body(x_ref, g_ref, win_ref, qn_ref, wuq_ref, wuqr_ref, kvn_ref, wuk_ref, wuv_ref,
               cos_ref, sin_ref, gb_ref, cw_ref, cb_ref,
               q_ref, k_ref, v_ref, qm_ref, kt_ref, vm_ref, om_ref, gc_ref, ubuf,
               *, rows, steps_per_seq):
    h = _rms(x_ref[...], g_ref[...]).astype(BF16)

    @pl.when(pl.program_id(0) % steps_per_seq == 0)
    def _():
        ubuf[0:SUBLANE, :] = jnp.zeros((SUBLANE, MQK_PAD), F32)

    ubuf[SUBLANE:SUBLANE + rows, :] = _dot(h, win_ref[:, Z_QKM:Z_VM])
    u = cb_ref[...]
    for j in range(CONV_K):
        start = SUBLANE - (CONV_K - 1) + j
        u = u + ubuf[start:start + rows, :] * cw_ref[j:j + 1, :]
    ubuf[0:SUBLANE, :] = ubuf[rows:rows + SUBLANE, :]
    qk = u * _sigmoid(u)
    upper = lax.broadcasted_iota(jnp.int32, (rows, HEAD_PAD), 1) >= MLSTM_QK_DIM
    for hd in range(MLSTM_HEADS):
        sl = slice(hd * HEAD_PAD, (hd + 1) * HEAD_PAD)
        swapped = pltpu.roll(qk[:, sl], MLSTM_QK_DIM, 1)
        qm_ref[:, sl] = jnp.where(upper, swapped * MLSTM_QK_DIM ** -0.5, 0.0).astype(BF16)
        kt_ref[sl, :] = qk[:, sl].T

    za = _dot(h, win_ref[:, Z_CQ:Z_QKM])
    cos = cos_ref[...]
    sin = sin_ref[...]
    cqn = _rms(za[:, Z_CQ:Z_CKV], qn_ref[...]).astype(BF16)
    ckvn = _rms(za[:, Z_CKV:Z_KR], kvn_ref[...]).astype(BF16)
    qa = _dot(cqn, wuq_ref[...])
    qb = _dot(cqn, wuqr_ref[...])
    kn = _dot(ckvn, wuk_ref[...])
    k_rope = za[:, Z_KR:Z_KRR] * cos + za[:, Z_KRR:Z_GATE] * sin
    vv = _dot(ckvn, wuv_ref[...])
    scale = Q_HEAD ** -0.5 * np.log2(np.e)
    ones_lane = lax.broadcasted_iota(jnp.int32, cos.shape, 1) >= V_HEAD_DIM
    for hd in range(MLA_HEADS):
        sl = slice(hd * HEAD_PAD, (hd + 1) * HEAD_PAD)
        q_ref[:, sl] = ((qa[:, sl] * cos + qb[:, sl] * sin) * scale).astype(BF16)
        k_ref[:, sl] = (kn[:, sl] + k_rope).astype(BF16)
        v_ref[:, sl] = jnp.where(ones_lane, 1.0, vv[:, sl]).astype(BF16)

    zb = _dot(h, win_ref[:, Z_VM:Z_WIDTH])
    vm_ref[...] = zb[:, :MLSTM_WIDTH].astype(BF16)
    om_ref[...] = zb[:, MLSTM_WIDTH:]
    gz = za[:, Z_GATE:Z_QKM] + gb_ref[...]
    lane = lax.broadcasted_iota(jnp.int32, gz.shape, 1)
    gc_ref[...] = jnp.where(lane < MLSTM_HEADS, gz, _log_sigmoid(gz))


def _proj(x, layer, w, cos, sin, seq_len):
    t = x.shape[0]
    rows = min(PROJ_ROWS, seq_len)

    def row_spec(width):
        return pl.BlockSpec((rows, width), lambda i: (i, 0))

    def const(shape):
        return pl.BlockSpec((None,) + shape, lambda i: (layer, 0, 0), pipeline_mode=pl.Buffered(1))

    out_widths = ((ATT_PAD, BF16), (ATT_PAD, BF16), (ATT_PAD, BF16), (MQK_PAD, BF16), None,
                  (MLSTM_WIDTH, BF16), (MLSTM_WIDTH, F32), (LANE, F32))
    kt_spec = pl.BlockSpec((MQK_PAD, rows), lambda i: (0, i))
    kt_shape = jax.ShapeDtypeStruct((MQK_PAD, t), F32)
    nbytes = (D_MODEL * Z_WIDTH * 2 + 2 * Q_LORA_RANK * ATT_PAD * 2 + 2 * KV_LORA_RANK * ATT_PAD * 2
              + rows * (Z_WIDTH * 4 + 4 * ATT_PAD * 4 + 2 * D_MODEL * 4 + 6 * MQK_PAD * 4)
              + 2 * rows * (MQK_PAD * 4 + sum(wd[0] * jnp.dtype(wd[1]).itemsize
                                              for wd in out_widths if wd is not None)))
    return pl.pallas_call(
        functools.partial(_proj_body, rows=rows, steps_per_seq=seq_len // rows),
        grid=(t // rows,),
        in_specs=[row_spec(D_MODEL), const((1, D_MODEL)), const((D_MODEL, Z_WIDTH)),
                  const((1, Q_LORA_RANK)), const((Q_LORA_RANK, ATT_PAD)), const((Q_LORA_RANK, ATT_PAD)),
                  const((1, KV_LORA_RANK)), const((KV_LORA_RANK, ATT_PAD)), const((KV_LORA_RANK, ATT_PAD)),
                  row_spec(LANE), row_spec(LANE), const((1, LANE)),
                  const((CONV_K, MQK_PAD)), const((1, MQK_PAD))],
        out_specs=[kt_spec if wd is None else row_spec(wd[0]) for wd in out_widths],
        out_shape=[kt_shape if wd is None else jax.ShapeDtypeStruct((t, wd[0]), wd[1])
                   for wd in out_widths],
        scratch_shapes=[pltpu.VMEM((rows + SUBLANE, MQK_PAD), F32)],
        compiler_params=pltpu.CompilerParams(
            dimension_semantics=("arbitrary",), vmem_limit_bytes=_vmem_limit(nbytes + (8 << 20))),
        name="in_proj",
    )(x, w["mix_norm"], w["w_in"], w["q_norm"], w["w_uq"], w["w_uq_rot"], w["kv_norm"],
      w["w_uk"], w["w_uv"], cos, sin, w["gate_bias"], w["conv_w"], w["conv_b"])


def _attn_body(q_ref, k_ref, v_ref, g_ref, o_ref, s_scr, p_scr, m_scr, acc_scr, fin_scr,
               *, blk, heads, nq):
    def rows(i):
        return pl.ds(pl.multiple_of(i * blk, blk), blk)

    def produce(hd, qi, j):
        sl = slice(hd * HEAD_PAD, (hd + 1) * HEAD_PAD)
        s_scr[hd] = lax.dot_general(q_ref[0, rows(qi), sl], k_ref[0, rows(j), sl],
                                    (((1,), (1,)), ((), ())), preferred_element_type=F32)

    half = blk // 2
    tri = (lax.broadcasted_iota(jnp.int32, (ATT_ROW_GROUP, ATT_ROW_GROUP), 1)
           <= lax.broadcasted_iota(jnp.int32, (ATT_ROW_GROUP, ATT_ROW_GROUP), 0))

    def consume(hd, j, diagonal=False):
        sl = slice(hd * HEAD_PAD, (hd + 1) * HEAD_PAD)
        for r0 in range(0, blk, ATT_ROW_GROUP):
            rg = slice(r0, r0 + ATT_ROW_GROUP)
            if diagonal:
                vis = r0 + ATT_ROW_GROUP
                ncol = half if r0 < half else blk
                own = jnp.where(tri, s_scr[hd, rg, r0:vis], -jnp.inf)
                s = own if r0 == 0 else jnp.concatenate([s_scr[hd, rg, :r0], own], axis=1)
            else:
                vis = ncol = blk
                s = s_scr[hd, rg]
            m_old = m_scr[hd, rg]
            m_new = jnp.maximum(m_old, jnp.max(s, axis=-1, keepdims=True))
            p_scr[hd, rg, :vis] = jnp.exp2(
                s - jnp.concatenate([m_new] * (vis // LANE), axis=1)).astype(BF16)
            if vis < ncol:
                p_scr[hd, rg, vis:ncol] = jnp.zeros((ATT_ROW_GROUP, ncol - vis), BF16)
            acc_scr[hd, rg] = jnp.exp2(m_old - m_new) * acc_scr[hd, rg]
            m_scr[hd, rg] = m_new
        if diagonal:
            top = pl.ds(pl.multiple_of(j * blk, blk), half)
            acc_scr[hd, :half] += _dot(p_scr[hd, :half, :half], v_ref[0, top, sl])
            acc_scr[hd, half:] += _dot(p_scr[hd, half:], v_ref[0, rows(j), sl])
        else:
            acc_scr[hd] += _dot(p_scr[hd], v_ref[0, rows(j), sl])

    def reset():
        m_scr[...] = jnp.full(m_scr.shape, -jnp.inf, F32)
        acc_scr[...] = jnp.zeros(acc_scr.shape, F32)

    def finish(hd, qi):
        sl = slice(hd * HEAD_PAD, (hd + 1) * HEAD_PAD)
        lane = lax.broadcasted_iota(jnp.int32, (blk, HEAD_PAD), 1)
        acc = fin_scr[hd]
        o = jnp.where(lane < V_HEAD_DIM, acc / pltpu.roll(acc, V_HEAD_DIM, 1), 0.0)
        ms = jnp.sum(o * o, axis=-1, keepdims=True) * (1.0 / V_HEAD_DIM)
        o_ref[0, rows(qi), sl] = (o * lax.rsqrt(ms + EPS) * g_ref[:, sl]).astype(BF16)

    reset()
    fin_scr[...] = jnp.ones(fin_scr.shape, F32)
    for hd in range(heads):
        produce(hd, 0, 0)

    def query_block(qi, carry):
        def step(j, c):
            for hd in range(heads):
                consume(hd, j)
                produce(hd, qi, j + 1)
            return c

        lax.fori_loop(0, qi, step, 0)
        prev = jnp.maximum(qi - 1, 0)
        nxt = jnp.minimum(qi + 1, nq - 1)
        for hd in range(heads):
            finish(hd, prev)
            consume(hd, qi, diagonal=True)
            produce(hd, nxt, 0)
        fin_scr[...] = acc_scr[...]
        reset()
        return carry

    lax.fori_loop(0, nq, query_block, 0)
    for hd in range(heads):
        finish(hd, nq - 1)


def _attention(q, k, v, g, layer):
    b, s, _ = q.shape
    blk = min(ATT_BLOCK, s)
    assert blk % (2 * ATT_ROW_GROUP) == 0 and ATT_ROW_GROUP == LANE
    width = ATT_HEADS_PER_STEP * HEAD_PAD
    seq_spec = pl.BlockSpec((1, s, width), lambda bi, hi: (bi, 0, hi))
    return pl.pallas_call(
        functools.partial(_attn_body, blk=blk, heads=ATT_HEADS_PER_STEP, nq=s // blk),
        grid=(b, MLA_HEADS // ATT_HEADS_PER_STEP),
        in_specs=[seq_spec, seq_spec, seq_spec,
                  pl.BlockSpec((None, 1, width), lambda bi, hi: (layer, 0, hi))],
        out_specs=seq_spec,
        out_shape=jax.ShapeDtypeStruct((b, s, ATT_PAD), BF16),
        scratch_shapes=[pltpu.VMEM((ATT_HEADS_PER_STEP, blk, blk), F32),
                        pltpu.VMEM((ATT_HEADS_PER_STEP, blk, blk), BF16),
                        pltpu.VMEM((ATT_HEADS_PER_STEP, blk, HEAD_PAD), F32),
                        pltpu.VMEM((ATT_HEADS_PER_STEP, blk, HEAD_PAD), F32),
                        pltpu.VMEM((ATT_HEADS_PER_STEP, blk, HEAD_PAD), F32)],
        compiler_params=pltpu.CompilerParams(
            dimension_semantics=("parallel", "parallel"),
            vmem_limit_bytes=_vmem_limit(10 * s * width * 2 + ATT_HEADS_PER_STEP * 8 * blk * blk * 4
                                         + (8 << 20))),
        name="mla_attention",
    )(q, k, v, g)


def _chunk_scan(x, chunk, op, fill):
    row = lax.broadcasted_iota(jnp.int32, x.shape, 0) % chunk
    d = 1
    while d < chunk:
        x = op(x, jnp.where(row >= d, pltpu.roll(x, d, 0), fill))
        d *= 2
    return x


def _split3(x):
    hi = x.astype(BF16)
    r1 = x - hi.astype(F32)
    mid = r1.astype(BF16)
    lo = (r1 - mid.astype(F32)).astype(BF16)
    return jnp.concatenate([hi, mid, lo], axis=-1)


def _mlstm_body(qm_ref, kt_ref, vm_ref, om_ref, gc_ref, rep_ref, hn_ref, y_ref, cst, mst,
                *, rows, chunk):
    @pl.when(pl.program_id(1) == 0)
    def _():
        cst[...] = jnp.zeros(cst.shape, F32)
        mst[...] = jnp.zeros(mst.shape, F32)

    gc = gc_ref[0]
    lane = lax.broadcasted_iota(jnp.int32, gc.shape, 1)
    bcum = _chunk_scan(gc, chunk, jnp.add, 0.0)
    a_col = gc - pltpu.roll(bcum, LANE - MLSTM_HEADS, 1)
    a_max = _chunk_scan(a_col, chunk, jnp.maximum, -jnp.inf)
    a_row = a_col.T
    rep_all = _dot(_split3(jnp.where(lane < MLSTM_HEADS, a_max, bcum)), rep_ref[...])

    tri = (lax.broadcasted_iota(jnp.int32, (chunk, chunk), 1)
           <= lax.broadcasted_iota(jnp.int32, (chunk, chunk), 0))
    ones = jnp.ones((chunk, LANE), BF16)

    for hd in range(MLSTM_HEADS):
        hs = slice(hd * HEAD_PAD, (hd + 1) * HEAD_PAD)
        vs = slice(hd * MLSTM_V_DIM, (hd + 1) * MLSTM_V_DIM)
        c_aug = cst[hd]
        m_prev = mst[hd:hd + 1, :]
        rep = rep_all[:, 2 * hd * LANE:2 * (hd + 1) * LANE]
        for c in range(rows // chunk):
            r = slice(c * chunk, (c + 1) * chunk)
            amax_rep = rep[r, :LANE]
            b_rep = rep[r, LANE:]
            a_r = a_row[hd:hd + 1, r]
            a_top = amax_rep[chunk - 1:chunk, :]
            b_end = b_rep[chunk - 1:chunk, :]

            big_m = jnp.maximum(m_prev, amax_rep)
            qc = qm_ref[0, r, hs]
            ktc = kt_ref[hs, r]
            p = jnp.exp(jnp.where(tri, a_r - big_m, -jnp.inf)) * _dot(qc, ktc.astype(BF16))
            e_inter = jnp.exp(m_prev - big_m)
            v_aug = jnp.concatenate([vm_ref[0, r, vs], ones], axis=-1)
            qc_state = _dot(qc, c_aug.astype(BF16))
            pv = _dot(p.astype(BF16), v_aug)
            num = e_inter * qc_state[:, :MLSTM_V_DIM] + pv[:, :MLSTM_V_DIM]
            den = e_inter * qc_state[:, MLSTM_V_DIM:] + pv[:, MLSTM_V_DIM:]
            hout = num / jnp.maximum(jnp.abs(den), jnp.exp(-(b_rep + big_m)))

            hn = _rms(hout, hn_ref[:, vs])
            y_ref[0, r, vs] = (_sigmoid(om_ref[0, r, vs]) * hn).astype(BF16)

            kv = _dot((ktc * jnp.exp(a_r - a_top)).astype(BF16), v_aug)
            m_top = jnp.maximum(m_prev, a_top)
            d_old = jnp.exp(m_prev - m_top)
            d_new = jnp.exp(a_top - m_top)
            c_aug = (jnp.concatenate([d_old, d_old], axis=1) * c_aug
                     + jnp.concatenate([d_new, d_new], axis=1) * kv)
            m_prev = b_end + m_top
        cst[hd] = c_aug
        mst[hd:hd + 1, :] = m_prev


def _mlstm(qm, kt, vm, om, gc, w, layer):
    b, s, _ = qm.shape
    rows = min(MLSTM_ROWS, s)
    chunk = min(MLSTM_CHUNK, rows)
    steps = s // rows

    def seq_spec(width):
        return pl.BlockSpec((1, rows, width), lambda bi, ti: (bi, ti, 0))

    return pl.pallas_call(
        functools.partial(_mlstm_body, rows=rows, chunk=chunk),
        grid=(b, steps),
        in_specs=[seq_spec(MQK_PAD), pl.BlockSpec((MQK_PAD, rows), lambda bi, ti: (0, bi * steps + ti)),
                  seq_spec(MLSTM_WIDTH), seq_spec(MLSTM_WIDTH), seq_spec(LANE),
                  pl.BlockSpec((3 * LANE, 2 * MQK_PAD), lambda bi, ti: (0, 0)),
                  pl.BlockSpec((None, 1, MLSTM_WIDTH), lambda bi, ti: (layer, 0, 0))],
        out_specs=seq_spec(MLSTM_WIDTH),
        out_shape=jax.ShapeDtypeStruct((b, s, MLSTM_WIDTH), BF16),
        scratch_shapes=[pltpu.VMEM((MLSTM_HEADS, HEAD_PAD, 2 * LANE), F32),
                        pltpu.VMEM((SUBLANE, LANE), F32)],
        compiler_params=pltpu.CompilerParams(
            dimension_semantics=("parallel", "arbitrary"),
            vmem_limit_bytes=_vmem_limit(rows * 2 * MQK_PAD * 4 * 8 + (8 << 20))),
        name="mlstm",
    )(qm, kt, vm, om, gc, w["gate_rep"], w["mlstm_norm"])


def _gather_cols(w, src, coef):
    return jnp.take(w, jnp.asarray(src, jnp.int32), axis=-1) * jnp.asarray(coef, w.dtype)


def _pad_heads(n_heads, width, base=0, stride=None):
    stride = width if stride is None else stride
    src = np.zeros(n_heads * HEAD_PAD, np.int64)
    coef = np.zeros(n_heads * HEAD_PAD, np.float32)
    for hd in range(n_heads):
        src[hd * HEAD_PAD:hd * HEAD_PAD + width] = base + hd * stride + np.arange(width)
        coef[hd * HEAD_PAD:hd * HEAD_PAD + width] = 1.0
    return src, coef


def _qk_pairs(q_base, k_base):
    d = np.arange(MLSTM_QK_DIM)
    return np.concatenate([np.concatenate([q_base + hd * MLSTM_QK_DIM + d, k_base + hd * MLSTM_QK_DIM + d])
                           for hd in range(MLSTM_HEADS)])


def _rot_half_cols(base):
    half = QK_ROPE_DIM // 2
    src = np.concatenate([base + half + np.arange(half), base + np.arange(half)])
    coef = np.concatenate([-np.ones(half, np.float32), np.ones(half, np.float32)])
    return src, coef


def _prepare_weights(p):
    off = np.concatenate([[0], np.cumsum(IN_SPLITS)])
    o_cq, o_ckv, o_kr, o_qm, o_km, o_vm, o_om, o_im, o_fm = (int(v) for v in off[:-1])

    src = np.zeros(Z_WIDTH, np.int64)
    coef = np.zeros(Z_WIDTH, np.float32)

    def put(dst, s, c):
        src[dst:dst + len(s)] = s
        coef[dst:dst + len(s)] = c

    put(Z_CQ, o_cq + np.arange(Q_LORA_RANK), 1.0)
    put(Z_CKV, o_ckv + np.arange(KV_LORA_RANK), 1.0)
    put(Z_KR + QK_NOPE_DIM, o_kr + np.arange(QK_ROPE_DIM), 1.0)
    put(Z_KRR + QK_NOPE_DIM, *_rot_half_cols(o_kr))
    qk_src = _qk_pairs(o_qm, o_km)
    put(Z_QKM, qk_src, 1.0)
    put(Z_VM, o_vm + np.arange(MLSTM_WIDTH), 1.0)
    put(Z_OM, o_om + np.arange(MLSTM_WIDTH), 1.0)
    put(Z_GATE, o_im + np.arange(MLSTM_HEADS), 1.0)
    put(Z_GATE + MLSTM_HEADS, o_fm + np.arange(MLSTM_HEADS), 1.0)
    w_in = _gather_cols(p["w_in"], src, coef).astype(BF16)

    q_src, q_coef = _pad_heads(MLA_HEADS, Q_HEAD)
    r_src = np.zeros(ATT_PAD, np.int64)
    r_coef = np.zeros(ATT_PAD, np.float32)
    for hd in range(MLA_HEADS):
        s, c = _rot_half_cols(hd * Q_HEAD + QK_NOPE_DIM)
        lo = hd * HEAD_PAD + QK_NOPE_DIM
        r_src[lo:lo + QK_ROPE_DIM] = s
        r_coef[lo:lo + QK_ROPE_DIM] = c
    k_src, k_coef = _pad_heads(MLA_HEADS, QK_NOPE_DIM, 0, KV_HEAD)
    v_src, v_coef = _pad_heads(MLA_HEADS, V_HEAD_DIM, QK_NOPE_DIM, KV_HEAD)

    conv_src = _qk_pairs(0, MLSTM_QK_WIDTH)
    conv_coef = np.ones(MQK_PAD, np.float32)

    a_src, a_coef = _pad_heads(MLA_HEADS, V_HEAD_DIM)
    depth = p["w_in"].shape[0]
    w_out_att = jnp.swapaxes(_gather_cols(jnp.swapaxes(p["w_out"][:, :MLA_WIDTH, :], 1, 2),
                                          a_src, a_coef), 1, 2)
    w_out = jnp.concatenate([w_out_att, p["w_out"][:, MLA_WIDTH:, :]], axis=1).astype(BF16)

    gate_bias = jnp.concatenate(
        [p["b_igate"], p["b_fgate"], jnp.zeros((depth, LANE - 2 * MLSTM_HEADS), F32)], axis=-1)

    def row(a):
        return a[:, None, :]

    gate_rep = np.zeros((3, LANE, 2 * MQK_PAD), np.float32)
    for hd in range(MLSTM_HEADS):
        gate_rep[:, hd, 2 * hd * LANE:(2 * hd + 1) * LANE] = 1.0
        gate_rep[:, MLSTM_HEADS + hd, (2 * hd + 1) * LANE:(2 * hd + 2) * LANE] = 1.0

    out = {
        "gate_rep": jnp.asarray(gate_rep.reshape(3 * LANE, 2 * MQK_PAD), BF16),
        "w_in": w_in,
        "mix_norm": row(p["mix_norm"]),
        "q_norm": row(p["q_latent_norm"]),
        "kv_norm": row(p["kv_latent_norm"]),
        "w_uq": _gather_cols(p["w_uq"], q_src, q_coef).astype(BF16),
        "w_uq_rot": _gather_cols(p["w_uq"], r_src, r_coef).astype(BF16),
        "w_uk": _gather_cols(p["w_ukv"], k_src, k_coef).astype(BF16),
        "w_uv": _gather_cols(p["w_ukv"], v_src, v_coef).astype(BF16),
        "gate_bias": row(gate_bias),
        "att_norm": row(_gather_cols(p["attn_head_norm"], a_src, a_coef)),
        "conv_w": _gather_cols(p["conv_w"], conv_src, conv_coef),
        "conv_b": row(_gather_cols(p["conv_b"], conv_src, conv_coef)),
        "mlstm_norm": row(p["mlstm_head_norm"]),
        "w_out": w_out,
    }
    for name in ("ffn1", "ffn2"):
        out[name + "_norm"] = row(p[name + "_norm"])
        for mat in ("w_gate", "w_up", "w_down"):
            out[f"{name}_{mat}"] = p[f"{name}_{mat}"].astype(BF16)
    return out


def kernel(x, positions, ffn1_norm, ffn1_w_gate, ffn1_w_up, ffn1_w_down, mix_norm, w_in, q_latent_norm, w_uq, kv_latent_norm, w_ukv, conv_w, conv_b, b_igate, b_fgate, attn_head_norm, mlstm_head_norm, w_out, ffn2_norm, ffn2_w_gate, ffn2_w_up, ffn2_w_down, final_norm):
    b, s, d = x.shape
    depth = w_in.shape[0]
    w = _prepare_weights(dict(
        ffn1_norm=ffn1_norm, ffn1_w_gate=ffn1_w_gate, ffn1_w_up=ffn1_w_up, ffn1_w_down=ffn1_w_down,
        mix_norm=mix_norm, w_in=w_in, q_latent_norm=q_latent_norm, w_uq=w_uq,
        kv_latent_norm=kv_latent_norm, w_ukv=w_ukv, conv_w=conv_w, conv_b=conv_b,
        b_igate=b_igate, b_fgate=b_fgate, attn_head_norm=attn_head_norm,
        mlstm_head_norm=mlstm_head_norm, w_out=w_out, ffn2_norm=ffn2_norm,
        ffn2_w_gate=ffn2_w_gate, ffn2_w_up=ffn2_w_up, ffn2_w_down=ffn2_w_down))
    cos, sin = _rope_tables(positions)

    t = b * s
    xt = x.reshape(t, d)
    for layer in range(depth):
        xt = _ffn(xt, layer, w["ffn1_norm"], w["ffn1_w_gate"], w["ffn1_w_up"], w["ffn1_w_down"])
        q, k, v, qm, kt, vm, om, gc = _proj(xt, layer, w, cos, sin, s)
        seq = lambda a: a.reshape(b, s, a.shape[-1])
        y_att = _attention(seq(q), seq(k), seq(v), w["att_norm"], layer)
        y_mem = _mlstm(seq(qm), kt, seq(vm), seq(om), seq(gc), w, layer)
        xt = _ffn(xt, layer, w["ffn2_norm"], w["ffn2_w_gate"], w["ffn2_w_up"], w["ffn2_w_down"],
                  mix=(y_att.reshape(t, ATT_PAD), y_mem.reshape(t, MLSTM_WIDTH), w["w_out"]),
                  final_norm=final_norm.reshape(1, d) if layer == depth - 1 else None)
    return xt.reshape(b, s, d)
```

```python
import functools

import jax
import jax.numpy as jnp
import numpy as np
from jax import lax
from jax.experimental import pallas as pl
from jax.experimental.pallas import tpu as pltpu

F32 = jnp.float32
BF16 = jnp.bfloat16

LANE = 128
SUBLANE = 8
V7X_VMEM_BYTES = 64 * 1024 * 1024

D_MODEL = 1024
MLA_HEADS = 8
QK_NOPE_DIM = 64
QK_ROPE_DIM = 32
V_HEAD_DIM = 64
Q_LORA_RANK = 256
KV_LORA_RANK = 128
ROPE_THETA = 10000.0
MLSTM_HEADS = 4
MLSTM_QK_DIM = 64
MLSTM_V_DIM = 128
CONV_K = 4
D_FF = 2816
EPS = 1e-6

MLA_WIDTH = MLA_HEADS * V_HEAD_DIM
MLSTM_WIDTH = MLSTM_HEADS * MLSTM_V_DIM
MLSTM_QK_WIDTH = MLSTM_HEADS * MLSTM_QK_DIM
Q_HEAD = QK_NOPE_DIM + QK_ROPE_DIM
KV_HEAD = QK_NOPE_DIM + V_HEAD_DIM
IN_SPLITS = (Q_LORA_RANK, KV_LORA_RANK, QK_ROPE_DIM, MLSTM_QK_WIDTH, MLSTM_QK_WIDTH,
             MLSTM_WIDTH, MLSTM_WIDTH, MLSTM_HEADS, MLSTM_HEADS)

HEAD_PAD = LANE
ATT_PAD = MLA_HEADS * HEAD_PAD
MQK_PAD = MLSTM_HEADS * HEAD_PAD
MIX_WIDTH = MLA_WIDTH + MLSTM_WIDTH
assert 2 * MLSTM_QK_DIM == HEAD_PAD and 2 * V_HEAD_DIM == HEAD_PAD

Z_CQ = 0
Z_CKV = Z_CQ + Q_LORA_RANK
Z_KR = Z_CKV + KV_LORA_RANK
Z_KRR = Z_KR + HEAD_PAD
Z_GATE = Z_KRR + HEAD_PAD
Z_QKM = Z_GATE + LANE
Z_VM = Z_QKM + MQK_PAD
Z_OM = Z_VM + MLSTM_WIDTH
Z_WIDTH = Z_OM + MLSTM_WIDTH

FFN_ROWS = 1024
FFN_CHUNKS = ((0, 768), (768, 1536), (1536, 2304), (2304, 2816))
PROJ_ROWS = 512
ATT_BLOCK = 512
ATT_HEADS_PER_STEP = 4
ATT_ROW_GROUP = 128
MLSTM_ROWS = 512
MLSTM_CHUNK = 128
ROPE_ROWS = 2048


def _vmem_limit(nbytes):
    return int(min(V7X_VMEM_BYTES - (4 << 20), max(nbytes, 32 << 20)))


def _rms(x, g):
    ms = jnp.mean(x * x, axis=-1, keepdims=True)
    return x * lax.rsqrt(ms + EPS) * g


def _sigmoid(x):
    return 1.0 / (1.0 + jnp.exp(-x))


def _dot(a, b):
    return jnp.dot(a, b, preferred_element_type=F32)


def _rope_body(pos_ref, inv_ref, cos_ref, sin_ref):
    ang = pos_ref[...].astype(F32) * inv_ref[...]
    lane = lax.broadcasted_iota(jnp.int32, ang.shape, 1)
    rope = (lane >= QK_NOPE_DIM) & (lane < Q_HEAD)
    cos_ref[...] = jnp.where(lane < QK_NOPE_DIM, 1.0, jnp.where(rope, jnp.cos(ang), 0.0))
    sin_ref[...] = jnp.where(rope, jnp.sin(ang), 0.0)


def _rope_tables(positions):
    n = positions.size
    rows = min(ROPE_ROWS, n)
    inv = ROPE_THETA ** (-jnp.arange(0, QK_ROPE_DIM, 2, dtype=F32) / QK_ROPE_DIM)
    inv_row = jnp.zeros((1, LANE), F32)
    inv_row = inv_row.at[0, QK_NOPE_DIM:Q_HEAD].set(jnp.concatenate([inv, inv]))
    spec = pl.BlockSpec((rows, LANE), lambda i: (i, 0))
    return pl.pallas_call(
        _rope_body,
        grid=(n // rows,),
        in_specs=[pl.BlockSpec((rows, 1), lambda i: (i, 0)),
                  pl.BlockSpec((1, LANE), lambda i: (0, 0))],
        out_specs=[spec, spec],
        out_shape=[jax.ShapeDtypeStruct((n, LANE), F32)] * 2,
        name="rope_tables",
    )(positions.reshape(n, 1), inv_row)


def _ffn_body(*refs, mix, final):
    refs = list(refs)
    o_ref = refs.pop()
    x = refs.pop(0)[...]
    if mix:
        ya_ref, ym_ref, wo_ref = refs[:3]
        refs = refs[3:]
        y = jnp.concatenate([ya_ref[...], ym_ref[...]], axis=-1)
        x = x + _dot(y, wo_ref[...])
    g_ref, wg_ref, wu_ref, wd_ref = refs[:4]
    h = _rms(x, g_ref[...]).astype(BF16)
    y = None
    for c0, c1 in FFN_CHUNKS:
        gate = _dot(h, wg_ref[:, c0:c1])
        up = _dot(h, wu_ref[:, c0:c1])
        act = (gate * _sigmoid(gate) * up).astype(BF16)
        part = _dot(act, wd_ref[c0:c1, :])
        y = part if y is None else y + part
    out = x + 0.5 * y
    if final:
        out = _rms(out, refs[4][...])
    o_ref[...] = out


def _ffn(x, layer, norm, wg, wu, wd, mix=None, final_norm=None):
    t = x.shape[0]
    rows = min(FFN_ROWS, t)

    def row_spec(w):
        return pl.BlockSpec((rows, w), lambda i: (i, 0))

    def const(shape):
        return pl.BlockSpec((None,) + shape, lambda i: (layer, 0, 0), pipeline_mode=pl.Buffered(1))

    args, specs = [x], [row_spec(D_MODEL)]
    nbytes = 4 * rows * D_MODEL * 4 + 3 * D_MODEL * D_FF * 2
    if mix is not None:
        args += list(mix)
        specs += [row_spec(MLA_WIDTH), row_spec(MLSTM_WIDTH), const((MIX_WIDTH, D_MODEL))]
        nbytes += MIX_WIDTH * (D_MODEL * 2 + rows * 2 * 3)
    args += [norm, wg, wu, wd]
    specs += [const((1, D_MODEL)), const((D_MODEL, D_FF)), const((D_MODEL, D_FF)),
              const((D_FF, D_MODEL))]
    if final_norm is not None:
        args.append(final_norm)
        specs.append(pl.BlockSpec((1, D_MODEL), lambda i: (0, 0)))
    widest = max(c1 - c0 for c0, c1 in FFN_CHUNKS)
    nbytes += rows * widest * (4 + 4 + 4 + 2) + rows * D_MODEL * 4 * 3
    return pl.pallas_call(
        functools.partial(_ffn_body, mix=mix is not None, final=final_norm is not None),
        grid=(t // rows,),
        in_specs=specs,
        out_specs=row_spec(D_MODEL),
        out_shape=jax.ShapeDtypeStruct((t, D_MODEL), F32),
        compiler_params=pltpu.CompilerParams(
            dimension_semantics=("parallel",), vmem_limit_bytes=_vmem_limit(nbytes + (8 << 20))),
        name="mix_ffn" if mix is not None else "ffn",
    )(*args)


def _log_sigmoid(x):
    return jnp.minimum(x, 0.0) - jnp.log1p(jnp.exp(-jnp.abs(x)))


def _proj_body(x_ref, g_ref, win_ref, qn_ref, wuq_ref, wuqr_ref, kvn_ref, wuk_ref, wuv_ref,
               cos_ref, sin_ref, gb_ref, cw_ref, cb_ref,
               q_ref, k_ref, v_ref, qm_ref, kt_ref, vm_ref, om_ref, gc_ref, ubuf,
               *, rows, steps_per_seq):
    h = _rms(x_ref[...], g_ref[...]).astype(BF16)

    @pl.when(pl.program_id(0) % steps_per_seq == 0)
    def _():
        ubuf[0:SUBLANE, :] = jnp.zeros((SUBLANE, MQK_PAD), F32)

    ubuf[SUBLANE:SUBLANE + rows, :] = _dot(h, win_ref[:, Z_QKM:Z_VM])
    u = cb_ref[...]
    for j in range(CONV_K):
        start = SUBLANE - (CONV_K - 1) + j
        u = u + ubuf[start:start + rows, :] * cw_ref[j:j + 1, :]
    ubuf[0:SUBLANE, :] = ubuf[rows:rows + SUBLANE, :]
    qk = u * _sigmoid(u)
    upper = lax.broadcasted_iota(jnp.int32, (rows, HEAD_PAD), 1) >= MLSTM_QK_DIM
    for hd in range(MLSTM_HEADS):
        sl = slice(hd * HEAD_PAD, (hd + 1) * HEAD_PAD)
        swapped = pltpu.roll(qk[:, sl], MLSTM_QK_DIM, 1)
        qm_ref[:, sl] = jnp.where(upper, swapped * MLSTM_QK_DIM ** -0.5, 0.0).astype(BF16)
        kt_ref[sl, :] = qk[:, sl].T

    za = _dot(h, win_ref[:, Z_CQ:Z_QKM])
    cos = cos_ref[...]
    sin = sin_ref[...]
    cqn = _rms(za[:, Z_CQ:Z_CKV], qn_ref[...]).astype(BF16)
    ckvn = _rms(za[:, Z_CKV:Z_KR], kvn_ref[...]).astype(BF16)
    qa = _dot(cqn, wuq_ref[...])
    qb = _dot(cqn, wuqr_ref[...])
    kn = _dot(ckvn, wuk_ref[...])
    k_rope = za[:, Z_KR:Z_KRR] * cos + za[:, Z_KRR:Z_GATE] * sin
    vv = _dot(ckvn, wuv_ref[...])
    scale = Q_HEAD ** -0.5 * np.log2(np.e)
    ones_lane = lax.broadcasted_iota(jnp.int32, cos.shape, 1) >= V_HEAD_DIM
    for hd in range(MLA_HEADS):
        sl = slice(hd * HEAD_PAD, (hd + 1) * HEAD_PAD)
        q_ref[:, sl] = ((qa[:, sl] * cos + qb[:, sl] * sin) * scale).astype(BF16)
        k_ref[:, sl] = (kn[:, sl] + k_rope).astype(BF16)
        v_ref[:, sl] = jnp.where(ones_lane, 1.0, vv[:, sl]).astype(BF16)

    zb = _dot(h, win_ref[:, Z_VM:Z_WIDTH])
    vm_ref[...] = zb[:, :MLSTM_WIDTH].astype(BF16)
    om_ref[...] = zb[:, MLSTM_WIDTH:]
    gz = za[:, Z_GATE:Z_QKM] + gb_ref[...]
    lane = lax.broadcasted_iota(jnp.int32, gz.shape, 1)
    gc_ref[...] = jnp.where(lane < MLSTM_HEADS, gz, _log_sigmoid(gz))


def _proj(x, layer, w, cos, sin, seq_len):
    t = x.shape[0]
    rows = min(PROJ_ROWS, seq_len)

    def row_spec(width):
        return pl.BlockSpec((rows, width), lambda i: (i, 0))

    def const(shape):
        return pl.BlockSpec((None,) + shape, lambda i: (layer, 0, 0), pipeline_mode=pl.Buffered(1))

    out_widths = ((ATT_PAD, BF16), (ATT_PAD, BF16), (ATT_PAD, BF16), (MQK_PAD, BF16), None,
                  (MLSTM_WIDTH, BF16), (MLSTM_WIDTH, F32), (LANE, F32))
    kt_spec = pl.BlockSpec((MQK_PAD, rows), lambda i: (0, i))
    kt_shape = jax.ShapeDtypeStruct((MQK_PAD, t), F32)
    nbytes = (D_MODEL * Z_WIDTH * 2 + 2 * Q_LORA_RANK * ATT_PAD * 2 + 2 * KV_LORA_RANK * ATT_PAD * 2
              + rows * (Z_WIDTH * 4 + 4 * ATT_PAD * 4 + 2 * D_MODEL * 4 + 6 * MQK_PAD * 4)
              + 2 * rows * (MQK_PAD * 4 + sum(wd[0] * jnp.dtype(wd[1]).itemsize
                                              for wd in out_widths if wd is not None)))
    return pl.pallas_call(
        functools.partial(_proj_body, rows=rows, steps_per_seq=seq_len // rows),
        grid=(t // rows,),
        in_specs=[row_spec(D_MODEL), const((1, D_MODEL)), const((D_MODEL, Z_WIDTH)),
                  const((1, Q_LORA_RANK)), const((Q_LORA_RANK, ATT_PAD)), const((Q_LORA_RANK, ATT_PAD)),
                  const((1, KV_LORA_RANK)), const((KV_LORA_RANK, ATT_PAD)), const((KV_LORA_RANK, ATT_PAD)),
                  row_spec(LANE), row_spec(LANE), const((1, LANE)),
                  const((CONV_K, MQK_PAD)), const((1, MQK_PAD))],
        out_specs=[kt_spec if wd is None else row_spec(wd[0]) for wd in out_widths],
        out_shape=[kt_shape if wd is None else jax.ShapeDtypeStruct((t, wd[0]), wd[1])
                   for wd in out_widths],
        scratch_shapes=[pltpu.VMEM((rows + SUBLANE, MQK_PAD), F32)],
        compiler_params=pltpu.CompilerParams(
            dimension_semantics=("arbitrary",), vmem_limit_bytes=_vmem_limit(nbytes + (8 << 20))),
        name="in_proj",
    )(x, w["mix_norm"], w["w_in"], w["q_norm"], w["w_uq"], w["w_uq_rot"], w["kv_norm"],
      w["w_uk"], w["w_uv"], cos, sin, w["gate_bias"], w["conv_w"], w["conv_b"])


def _attn_body(q_ref, k_ref, v_ref, g_ref, o_ref, s_scr, p_scr, m_scr, acc_scr, fin_scr,
               *, blk, heads, nq):
    def rows(i):
        return pl.ds(pl.multiple_of(i * blk, blk), blk)

    def produce(hd, qi, j):
        sl = slice(hd * HEAD_PAD, (hd + 1) * HEAD_PAD)
        s_scr[hd] = lax.dot_general(q_ref[0, rows(qi), sl], k_ref[0, rows(j), sl],
                                    (((1,), (1,)), ((), ())), preferred_element_type=F32)

    half = blk // 2
    tri = (lax.broadcasted_iota(jnp.int32, (ATT_ROW_GROUP, ATT_ROW_GROUP), 1)
           <= lax.broadcasted_iota(jnp.int32, (ATT_ROW_GROUP, ATT_ROW_GROUP), 0))

    def consume(hd, j, diagonal=False):
        sl = slice(hd * HEAD_PAD, (hd + 1) * HEAD_PAD)
        for r0 in range(0, blk, ATT_ROW_GROUP):
            rg = slice(r0, r0 + ATT_ROW_GROUP)
            if diagonal:
                vis = r0 + ATT_ROW_GROUP
                ncol = half if r0 < half else blk
                own = jnp.where(tri, s_scr[hd, rg, r0:vis], -jnp.inf)
                s = own if r0 == 0 else jnp.concatenate([s_scr[hd, rg, :r0], own], axis=1)
            else:
                vis = ncol = blk
                s = s_scr[hd, rg]
            m_old = m_scr[hd, rg]
            m_new = jnp.maximum(m_old, jnp.max(s, axis=-1, keepdims=True))
            p_scr[hd, rg, :vis] = jnp.exp2(
                s - jnp.concatenate([m_new] * (vis // LANE), axis=1)).astype(BF16)
            if vis < ncol:
                p_scr[hd, rg, vis:ncol] = jnp.zeros((ATT_ROW_GROUP, ncol - vis), BF16)
            acc_scr[hd, rg] = jnp.exp2(m_old - m_new) * acc_scr[hd, rg]
            m_scr[hd, rg] = m_new
        if diagonal:
            top = pl.ds(pl.multiple_of(j * blk, blk), half)
            acc_scr[hd, :half] += _dot(p_scr[hd, :half, :half], v_ref[0, top, sl])
            acc_scr[hd, half:] += _dot(p_scr[hd, half:], v_ref[0, rows(j), sl])
        else:
            acc_scr[hd] += _dot(p_scr[hd], v_ref[0, rows(j), sl])

    def reset():
        m_scr[...] = jnp.full(m_scr.shape, -jnp.inf, F32)
        acc_scr[...] = jnp.zeros(acc_scr.shape, F32)

    def finish(pair, qi):
        lane = lax.broadcasted_iota(jnp.int32, (blk, HEAD_PAD), 1)
        normed = []
        for hd in (2 * pair, 2 * pair + 1):
            acc = fin_scr[hd]
            o = jnp.where(lane < V_HEAD_DIM, acc / pltpu.roll(acc, V_HEAD_DIM, 1), 0.0)
            ms = jnp.sum(o * o, axis=-1, keepdims=True) * (1.0 / V_HEAD_DIM)
            normed.append(o * lax.rsqrt(ms + EPS))
        sl = slice(pair * HEAD_PAD, (pair + 1) * HEAD_PAD)
        both = normed[0] + pltpu.roll(normed[1], V_HEAD_DIM, 1)
        o_ref[0, rows(qi), sl] = (both * g_ref[:, sl]).astype(BF16)

    reset()
    fin_scr[...] = jnp.ones(fin_scr.shape, F32)
    for hd in range(heads):
        produce(hd, 0, 0)

    def query_block(qi, carry):
        def step(j, c):
            for hd in range(heads):
                consume(hd, j)
                produce(hd, qi, j + 1)
            return c

        lax.fori_loop(0, qi, step, 0)
        prev = jnp.maximum(qi - 1, 0)
        nxt = jnp.minimum(qi + 1, nq - 1)
        for hd in range(heads):
            if hd % 2 == 0:
                finish(hd // 2, prev)
            consume(hd, qi, diagonal=True)
            produce(hd, nxt, 0)
        fin_scr[...] = acc_scr[...]
        reset()
        return carry

    lax.fori_loop(0, nq, query_block, 0)
    for pair in range(heads // 2):
        finish(pair, nq - 1)


def _attention(q, k, v, g, layer):
    b, s, _ = q.shape
    blk = min(ATT_BLOCK, s)
    assert blk % (2 * ATT_ROW_GROUP) == 0 and ATT_ROW_GROUP == LANE
    width = ATT_HEADS_PER_STEP * HEAD_PAD
    out_width = ATT_HEADS_PER_STEP * V_HEAD_DIM
    seq_spec = pl.BlockSpec((1, s, width), lambda bi, hi: (bi, 0, hi))
    return pl.pallas_call(
        functools.partial(_attn_body, blk=blk, heads=ATT_HEADS_PER_STEP, nq=s // blk),
        grid=(b, MLA_HEADS // ATT_HEADS_PER_STEP),
        in_specs=[seq_spec, seq_spec, seq_spec,
                  pl.BlockSpec((None, 1, out_width), lambda bi, hi: (layer, 0, hi))],
        out_specs=pl.BlockSpec((1, s, out_width), lambda bi, hi: (bi, 0, hi)),
        out_shape=jax.ShapeDtypeStruct((b, s, MLA_WIDTH), BF16),
        scratch_shapes=[pltpu.VMEM((ATT_HEADS_PER_STEP, blk, blk), F32),
                        pltpu.VMEM((ATT_HEADS_PER_STEP, blk, blk), BF16),
                        pltpu.VMEM((ATT_HEADS_PER_STEP, blk, HEAD_PAD), F32),
                        pltpu.VMEM((ATT_HEADS_PER_STEP, blk, HEAD_PAD), F32),
                        pltpu.VMEM((ATT_HEADS_PER_STEP, blk, HEAD_PAD), F32)],
        compiler_params=pltpu.CompilerParams(
            dimension_semantics=("parallel", "parallel"),
            vmem_limit_bytes=_vmem_limit(10 * s * width * 2 + ATT_HEADS_PER_STEP * 8 * blk * blk * 4
                                         + (8 << 20))),
        name="mla_attention",
    )(q, k, v, g)


def _chunk_scan(x, chunk, op, fill):
    row = lax.broadcasted_iota(jnp.int32, x.shape, 0) % chunk
    d = 1
    while d < chunk:
        x = op(x, jnp.where(row >= d, pltpu.roll(x, d, 0), fill))
        d *= 2
    return x


def _split3(x):
    hi = x.astype(BF16)
    r1 = x - hi.astype(F32)
    mid = r1.astype(BF16)
    lo = (r1 - mid.astype(F32)).astype(BF16)
    return jnp.concatenate([hi, mid, lo], axis=-1)


def _mlstm_body(qm_ref, kt_ref, vm_ref, om_ref, gc_ref, rep_ref, hn_ref, y_ref, cst, mst,
                *, rows, chunk):
    @pl.when(pl.program_id(1) == 0)
    def _():
        cst[...] = jnp.zeros(cst.shape, F32)
        mst[...] = jnp.zeros(mst.shape, F32)

    gc = gc_ref[0]
    lane = lax.broadcasted_iota(jnp.int32, gc.shape, 1)
    bcum = _chunk_scan(gc, chunk, jnp.add, 0.0)
    a_col = gc - pltpu.roll(bcum, LANE - MLSTM_HEADS, 1)
    a_max = _chunk_scan(a_col, chunk, jnp.maximum, -jnp.inf)
    a_row = a_col.T
    rep_all = _dot(_split3(jnp.where(lane < MLSTM_HEADS, a_max, bcum)), rep_ref[...])

    tri = (lax.broadcasted_iota(jnp.int32, (chunk, chunk), 1)
           <= lax.broadcasted_iota(jnp.int32, (chunk, chunk), 0))
    ones = jnp.ones((chunk, LANE), BF16)

    for hd in range(MLSTM_HEADS):
        hs = slice(hd * HEAD_PAD, (hd + 1) * HEAD_PAD)
        vs = slice(hd * MLSTM_V_DIM, (hd + 1) * MLSTM_V_DIM)
        c_aug = cst[hd]
        m_prev = mst[hd:hd + 1, :]
        rep = rep_all[:, 2 * hd * LANE:2 * (hd + 1) * LANE]
        for c in range(rows // chunk):
            r = slice(c * chunk, (c + 1) * chunk)
            amax_rep = rep[r, :LANE]
            b_rep = rep[r, LANE:]
            a_r = a_row[hd:hd + 1, r]
            a_top = amax_rep[chunk - 1:chunk, :]
            b_end = b_rep[chunk - 1:chunk, :]

            big_m = jnp.maximum(m_prev, amax_rep)
            qc = qm_ref[0, r, hs]
            ktc = kt_ref[hs, r]
            p = jnp.exp(jnp.where(tri, a_r - big_m, -jnp.inf)) * _dot(qc, ktc.astype(BF16))
            e_inter = jnp.exp(m_prev - big_m)
            v_aug = jnp.concatenate([vm_ref[0, r, vs], ones], axis=-1)
            qc_state = _dot(qc, c_aug.astype(BF16))
            pv = _dot(p.astype(BF16), v_aug)
            num = e_inter * qc_state[:, :MLSTM_V_DIM] + pv[:, :MLSTM_V_DIM]
            den = e_inter * qc_state[:, MLSTM_V_DIM:] + pv[:, MLSTM_V_DIM:]
            hout = num / jnp.maximum(jnp.abs(den), jnp.exp(-(b_rep + big_m)))

            hn = _rms(hout, hn_ref[:, vs])
            y_ref[0, r, vs] = (_sigmoid(om_ref[0, r, vs]) * hn).astype(BF16)

            kv = _dot((ktc * jnp.exp(a_r - a_top)).astype(BF16), v_aug)
            m_top = jnp.maximum(m_prev, a_top)
            d_old = jnp.exp(m_prev - m_top)
            d_new = jnp.exp(a_top - m_top)
            c_aug = (jnp.concatenate([d_old, d_old], axis=1) * c_aug
                     + jnp.concatenate([d_new, d_new], axis=1) * kv)
            m_prev = b_end + m_top
        cst[hd] = c_aug
        mst[hd:hd + 1, :] = m_prev


def _mlstm(qm, kt, vm, om, gc, w, layer):
    b, s, _ = qm.shape
    rows = min(MLSTM_ROWS, s)
    chunk = min(MLSTM_CHUNK, rows)
    steps = s // rows

    def seq_spec(width):
        return pl.BlockSpec((1, rows, width), lambda bi, ti: (bi, ti, 0))

    return pl.pallas_call(
        functools.partial(_mlstm_body, rows=rows, chunk=chunk),
        grid=(b, steps),
        in_specs=[seq_spec(MQK_PAD), pl.BlockSpec((MQK_PAD, rows), lambda bi, ti: (0, bi * steps + ti)),
                  seq_spec(MLSTM_WIDTH), seq_spec(MLSTM_WIDTH), seq_spec(LANE),
                  pl.BlockSpec((3 * LANE, 2 * MQK_PAD), lambda bi, ti: (0, 0)),
                  pl.BlockSpec((None, 1, MLSTM_WIDTH), lambda bi, ti: (layer, 0, 0))],
        out_specs=seq_spec(MLSTM_WIDTH),
        out_shape=jax.ShapeDtypeStruct((b, s, MLSTM_WIDTH), BF16),
        scratch_shapes=[pltpu.VMEM((MLSTM_HEADS, HEAD_PAD, 2 * LANE), F32),
                        pltpu.VMEM((SUBLANE, LANE), F32)],
        compiler_params=pltpu.CompilerParams(
            dimension_semantics=("parallel", "arbitrary"),
            vmem_limit_bytes=_vmem_limit(rows * 2 * MQK_PAD * 4 * 8 + (8 << 20))),
        name="mlstm",
    )(qm, kt, vm, om, gc, w["gate_rep"], w["mlstm_norm"])


def _gather_cols(w, src, coef):
    parts = []
    i, n = 0, len(src)
    while i < n:
        j = i + 1
        while j < n and coef[j] == coef[i] and (coef[i] == 0 or src[j] == src[j - 1] + 1):
            j += 1
        if coef[i] == 0:
            parts.append(jnp.zeros(w.shape[:-1] + (j - i,), w.dtype))
        else:
            piece = lax.slice_in_dim(w, int(src[i]), int(src[i]) + (j - i), axis=w.ndim - 1)
            parts.append(piece if coef[i] == 1 else piece * float(coef[i]))
        i = j
    return jnp.concatenate(parts, axis=-1)


def _pad_heads(n_heads, width, base=0, stride=None):
    stride = width if stride is None else stride
    src = np.zeros(n_heads * HEAD_PAD, np.int64)
    coef = np.zeros(n_heads * HEAD_PAD, np.float32)
    for hd in range(n_heads):
        src[hd * HEAD_PAD:hd * HEAD_PAD + width] = base + hd * stride + np.arange(width)
        coef[hd * HEAD_PAD:hd * HEAD_PAD + width] = 1.0
    return src, coef


def _qk_pairs(q_base, k_base):
    d = np.arange(MLSTM_QK_DIM)
    return np.concatenate([np.concatenate([q_base + hd * MLSTM_QK_DIM + d, k_base + hd * MLSTM_QK_DIM + d])
                           for hd in range(MLSTM_HEADS)])


def _rot_half_cols(base):
    half = QK_ROPE_DIM // 2
    src = np.concatenate([base + half + np.arange(half), base + np.arange(half)])
    coef = np.concatenate([-np.ones(half, np.float32), np.ones(half, np.float32)])
    return src, coef


def _prepare_weights(p):
    off = np.concatenate([[0], np.cumsum(IN_SPLITS)])
    o_cq, o_ckv, o_kr, o_qm, o_km, o_vm, o_om, o_im, o_fm = (int(v) for v in off[:-1])

    src = np.zeros(Z_WIDTH, np.int64)
    coef = np.zeros(Z_WIDTH, np.float32)

    def put(dst, s, c):
        src[dst:dst + len(s)] = s
        coef[dst:dst + len(s)] = c

    put(Z_CQ, o_cq + np.arange(Q_LORA_RANK), 1.0)
    put(Z_CKV, o_ckv + np.arange(KV_LORA_RANK), 1.0)
    put(Z_KR + QK_NOPE_DIM, o_kr + np.arange(QK_ROPE_DIM), 1.0)
    put(Z_KRR + QK_NOPE_DIM, *_rot_half_cols(o_kr))
    qk_src = _qk_pairs(o_qm, o_km)
    put(Z_QKM, qk_src, 1.0)
    put(Z_VM, o_vm + np.arange(MLSTM_WIDTH), 1.0)
    put(Z_OM, o_om + np.arange(MLSTM_WIDTH), 1.0)
    put(Z_GATE, o_im + np.arange(MLSTM_HEADS), 1.0)
    put(Z_GATE + MLSTM_HEADS, o_fm + np.arange(MLSTM_HEADS), 1.0)
    w_in = _gather_cols(p["w_in"], src, coef).astype(BF16)

    q_src, q_coef = _pad_heads(MLA_HEADS, Q_HEAD)
    r_src = np.zeros(ATT_PAD, np.int64)
    r_coef = np.zeros(ATT_PAD, np.float32)
    for hd in range(MLA_HEADS):
        s, c = _rot_half_cols(hd * Q_HEAD + QK_NOPE_DIM)
        lo = hd * HEAD_PAD + QK_NOPE_DIM
        r_src[lo:lo + QK_ROPE_DIM] = s
        r_coef[lo:lo + QK_ROPE_DIM] = c
    k_src, k_coef = _pad_heads(MLA_HEADS, QK_NOPE_DIM, 0, KV_HEAD)
    v_src, v_coef = _pad_heads(MLA_HEADS, V_HEAD_DIM, QK_NOPE_DIM, KV_HEAD)

    conv_src = _qk_pairs(0, MLSTM_QK_WIDTH)
    conv_coef = np.ones(MQK_PAD, np.float32)

    depth = p["w_in"].shape[0]

    gate_bias = jnp.concatenate(
        [p["b_igate"], p["b_fgate"], jnp.zeros((depth, LANE - 2 * MLSTM_HEADS), F32)], axis=-1)

    def row(a):
        return a[:, None, :]

    gate_rep = np.zeros((3, LANE, 2 * MQK_PAD), np.float32)
    for hd in range(MLSTM_HEADS):
        gate_rep[:, hd, 2 * hd * LANE:(2 * hd + 1) * LANE] = 1.0
        gate_rep[:, MLSTM_HEADS + hd, (2 * hd + 1) * LANE:(2 * hd + 2) * LANE] = 1.0

    out = {
        "gate_rep": jnp.asarray(gate_rep.reshape(3 * LANE, 2 * MQK_PAD), BF16),
        "w_in": w_in,
        "mix_norm": row(p["mix_norm"]),
        "q_norm": row(p["q_latent_norm"]),
        "kv_norm": row(p["kv_latent_norm"]),
        "w_uq": _gather_cols(p["w_uq"], q_src, q_coef).astype(BF16),
        "w_uq_rot": _gather_cols(p["w_uq"], r_src, r_coef).astype(BF16),
        "w_uk": _gather_cols(p["w_ukv"], k_src, k_coef).astype(BF16),
        "w_uv": _gather_cols(p["w_ukv"], v_src, v_coef).astype(BF16),
        "gate_bias": row(gate_bias),
        "att_norm": row(p["attn_head_norm"]),
        "conv_w": _gather_cols(p["conv_w"], conv_src, conv_coef),
        "conv_b": row(_gather_cols(p["conv_b"], conv_src, conv_coef)),
        "mlstm_norm": row(p["mlstm_head_norm"]),
        "w_out": p["w_out"].astype(BF16),
    }
    for name in ("ffn1", "ffn2"):
        out[name + "_norm"] = row(p[name + "_norm"])
        for mat in ("w_gate", "w_up", "w_down"):
            out[f"{name}_{mat}"] = p[f"{name}_{mat}"].astype(BF16)
    return out


def kernel(x, positions, ffn1_norm, ffn1_w_gate, ffn1_w_up, ffn1_w_down, mix_norm, w_in, q_latent_norm, w_uq, kv_latent_norm, w_ukv, conv_w, conv_b, b_igate, b_fgate, attn_head_norm, mlstm_head_norm, w_out, ffn2_norm, ffn2_w_gate, ffn2_w_up, ffn2_w_down, final_norm):
    b, s, d = x.shape
    depth = w_in.shape[0]
    w = _prepare_weights(dict(
        ffn1_norm=ffn1_norm, ffn1_w_gate=ffn1_w_gate, ffn1_w_up=ffn1_w_up, ffn1_w_down=ffn1_w_down,
        mix_norm=mix_norm, w_in=w_in, q_latent_norm=q_latent_norm, w_uq=w_uq,
        kv_latent_norm=kv_latent_norm, w_ukv=w_ukv, conv_w=conv_w, conv_b=conv_b,
        b_igate=b_igate, b_fgate=b_fgate, attn_head_norm=attn_head_norm,
        mlstm_head_norm=mlstm_head_norm, w_out=w_out, ffn2_norm=ffn2_norm,
        ffn2_w_gate=ffn2_w_gate, ffn2_w_up=ffn2_w_up, ffn2_w_down=ffn2_w_down))
    cos, sin = _rope_tables(positions)

    t = b * s
    xt = x.reshape(t, d)
    for layer in range(depth):
        xt = _ffn(xt, layer, w["ffn1_norm"], w["ffn1_w_gate"], w["ffn1_w_up"], w["ffn1_w_down"])
        q, k, v, qm, kt, vm, om, gc = _proj(xt, layer, w, cos, sin, s)
        seq = lambda a: a.reshape(b, s, a.shape[-1])
        y_att = _attention(seq(q), seq(k), seq(v), w["att_norm"], layer)
        y_mem = _mlstm(seq(qm), kt, seq(vm), seq(om), seq(gc), w, layer)
        xt = _ffn(xt, layer, w["ffn2_norm"], w["ffn2_w_gate"], w["ffn2_w_up"], w["ffn2_w_down"],
                  mix=(y_att.reshape(t, MLA_WIDTH), y_mem.reshape(t, MLSTM_WIDTH), w["w_out"]),
                  final_norm=final_norm.reshape(1, d) if layer == depth - 1 else None)
    return xt.reshape(b, s, d)
```

```python
import functools

import jax
import jax.numpy as jnp
import numpy as np
from jax import lax
from jax.experimental import pallas as pl
from jax.experimental.pallas import tpu as pltpu

F32 = jnp.float32
BF16 = jnp.bfloat16

LANE = 128
SUBLANE = 8
V7X_VMEM_BYTES = 64 * 1024 * 1024

D_MODEL = 1024
MLA_HEADS = 8
QK_NOPE_DIM = 64
QK_ROPE_DIM = 32
V_HEAD_DIM = 64
Q_LORA_RANK = 256
KV_LORA_RANK = 128
ROPE_THETA = 10000.0
MLSTM_HEADS = 4
MLSTM_QK_DIM = 64
MLSTM_V_DIM = 128
CONV_K = 4
D_FF = 2816
EPS = 1e-6

MLA_WIDTH = MLA_HEADS * V_HEAD_DIM
MLSTM_WIDTH = MLSTM_HEADS * MLSTM_V_DIM
MLSTM_QK_WIDTH = MLSTM_HEADS * MLSTM_QK_DIM
Q_HEAD = QK_NOPE_DIM + QK_ROPE_DIM
KV_HEAD = QK_NOPE_DIM + V_HEAD_DIM
IN_SPLITS = (Q_LORA_RANK, KV_LORA_RANK, QK_ROPE_DIM, MLSTM_QK_WIDTH, MLSTM_QK_WIDTH,
             MLSTM_WIDTH, MLSTM_WIDTH, MLSTM_HEADS, MLSTM_HEADS)

HEAD_PAD = LANE
ATT_PAD = MLA_HEADS * HEAD_PAD
MQK_PAD = MLSTM_HEADS * HEAD_PAD
MIX_WIDTH = MLA_WIDTH + MLSTM_WIDTH
assert 2 * MLSTM_QK_DIM == HEAD_PAD and 2 * V_HEAD_DIM == HEAD_PAD

Z_CQ = 0
Z_CKV = Z_CQ + Q_LORA_RANK
Z_KR = Z_CKV + KV_LORA_RANK
Z_KRR = Z_KR + HEAD_PAD
Z_GATE = Z_KRR + HEAD_PAD
Z_QKM = Z_GATE + LANE
Z_VM = Z_QKM + MQK_PAD
Z_OM = Z_VM + MLSTM_WIDTH
Z_WIDTH = Z_OM + MLSTM_WIDTH

FFN_ROWS = 1024
FFN_CHUNKS = ((0, 768), (768, 1536), (1536, 2304), (2304, 2816))
PROJ_ROWS = 512
ATT_BLOCK = 512
ATT_HEADS_PER_STEP = 4
ATT_ROW_GROUP = 128
MLSTM_ROWS = 512
MLSTM_CHUNK = 128
ROPE_ROWS = 2048


def _vmem_limit(nbytes):
    return int(min(V7X_VMEM_BYTES - (4 << 20), max(nbytes, 32 << 20)))


def _rms(x, g):
    ms = jnp.mean(x * x, axis=-1, keepdims=True)
    return x * lax.rsqrt(ms + EPS) * g


def _sigmoid(x):
    return 1.0 / (1.0 + jnp.exp(-x))


def _dot(a, b):
    return jnp.dot(a, b, preferred_element_type=F32)


def _rope_body(pos_ref, inv_ref, cos_ref, sin_ref):
    groups = LANE // QK_ROPE_DIM
    quarter = pos_ref.shape[0] // groups
    lane = lax.broadcasted_iota(jnp.int32, (quarter, LANE), 1)
    pos = pos_ref[(groups - 1) * quarter:, :].astype(F32)
    for k in range(groups - 2, -1, -1):
        pos = jnp.where(lane < (k + 1) * QK_ROPE_DIM, pos_ref[k * quarter:(k + 1) * quarter, :].astype(F32),
                        pos)
    ang = pos * inv_ref[...]
    cos, sin = jnp.cos(ang), jnp.sin(ang)
    rope = (lane >= QK_NOPE_DIM) & (lane < Q_HEAD)
    for k in range(groups):
        shift = (QK_NOPE_DIM - k * QK_ROPE_DIM) % LANE
        ck = cos if shift == 0 else pltpu.roll(cos, shift, 1)
        sk = sin if shift == 0 else pltpu.roll(sin, shift, 1)
        out = slice(k * quarter, (k + 1) * quarter)
        cos_ref[out, :] = jnp.where(lane < QK_NOPE_DIM, 1.0, jnp.where(rope, ck, 0.0))
        sin_ref[out, :] = jnp.where(rope, sk, 0.0)


def _rope_tables(positions):
    n = positions.size
    rows = min(ROPE_ROWS, n)
    inv = ROPE_THETA ** (-jnp.arange(0, QK_ROPE_DIM, 2, dtype=F32) / QK_ROPE_DIM)
    inv_row = jnp.tile(jnp.concatenate([inv, inv]), LANE // QK_ROPE_DIM).reshape(1, LANE)
    spec = pl.BlockSpec((rows, LANE), lambda i: (i, 0))
    return pl.pallas_call(
        _rope_body,
        grid=(n // rows,),
        in_specs=[pl.BlockSpec((rows, 1), lambda i: (i, 0)),
                  pl.BlockSpec((1, LANE), lambda i: (0, 0))],
        out_specs=[spec, spec],
        out_shape=[jax.ShapeDtypeStruct((n, LANE), F32)] * 2,
        name="rope_tables",
    )(positions.reshape(n, 1), inv_row)


def _ffn_body(*refs, mix, final):
    refs = list(refs)
    o_ref = refs.pop()
    x = refs.pop(0)[...]
    if mix:
        ya_ref, ym_ref, wo_ref = refs[:3]
        refs = refs[3:]
        y = jnp.concatenate([ya_ref[...], ym_ref[...]], axis=-1)
        x = x + _dot(y, wo_ref[...])
    g_ref, wg_ref, wu_ref, wd_ref = refs[:4]
    h = _rms(x, g_ref[...]).astype(BF16)
    y = None
    for c0, c1 in FFN_CHUNKS:
        gate = _dot(h, wg_ref[:, c0:c1])
        up = _dot(h, wu_ref[:, c0:c1])
        act = (gate * _sigmoid(gate) * up).astype(BF16)
        part = _dot(act, wd_ref[c0:c1, :])
        y = part if y is None else y + part
    out = x + 0.5 * y
    if final:
        out = _rms(out, refs[4][...])
    o_ref[...] = out


def _ffn(x, layer, norm, wg, wu, wd, mix=None, final_norm=None):
    t = x.shape[0]
    rows = min(FFN_ROWS, t)

    def row_spec(w):
        return pl.BlockSpec((rows, w), lambda i: (i, 0))

    def const(shape):
        return pl.BlockSpec((None,) + shape, lambda i: (layer, 0, 0), pipeline_mode=pl.Buffered(1))

    args, specs = [x], [row_spec(D_MODEL)]
    nbytes = 4 * rows * D_MODEL * 4 + 3 * D_MODEL * D_FF * 2
    if mix is not None:
        args += list(mix)
        specs += [row_spec(MLA_WIDTH), row_spec(MLSTM_WIDTH), const((MIX_WIDTH, D_MODEL))]
        nbytes += MIX_WIDTH * (D_MODEL * 2 + rows * 2 * 3)
    args += [norm, wg, wu, wd]
    specs += [const((1, D_MODEL)), const((D_MODEL, D_FF)), const((D_MODEL, D_FF)),
              const((D_FF, D_MODEL))]
    if final_norm is not None:
        args.append(final_norm)
        specs.append(pl.BlockSpec((1, D_MODEL), lambda i: (0, 0)))
    widest = max(c1 - c0 for c0, c1 in FFN_CHUNKS)
    nbytes += rows * widest * (4 + 4 + 4 + 2) + rows * D_MODEL * 4 * 3
    return pl.pallas_call(
        functools.partial(_ffn_body, mix=mix is not None, final=final_norm is not None),
        grid=(t // rows,),
        in_specs=specs,
        out_specs=row_spec(D_MODEL),
        out_shape=jax.ShapeDtypeStruct((t, D_MODEL), F32),
        compiler_params=pltpu.CompilerParams(
            dimension_semantics=("parallel",), vmem_limit_bytes=_vmem_limit(nbytes + (8 << 20))),
        name="mix_ffn" if mix is not None else "ffn",
    )(*args)


def _log_sigmoid(x):
    return jnp.minimum(x, 0.0) - jnp.log1p(jnp.exp(-jnp.abs(x)))


def _proj_body(x_ref, g_ref, win_ref, qn_ref, wuq_ref, wuqr_ref, kvn_ref, wuk_ref, wuv_ref,
               cos_ref, sin_ref, gb_ref, cw_ref, cb_ref,
               q_ref, k_ref, v_ref, qm_ref, kt_ref, vm_ref, om_ref, gc_ref, ubuf,
               *, rows, steps_per_seq):
    h = _rms(x_ref[...], g_ref[...]).astype(BF16)

    @pl.when(pl.program_id(0) % steps_per_seq == 0)
    def _():
        ubuf[0:SUBLANE, :] = jnp.zeros((SUBLANE, MQK_PAD), F32)

    ubuf[SUBLANE:SUBLANE + rows, :] = _dot(h, win_ref[:, Z_QKM:Z_VM])
    u = cb_ref[...]
    for j in range(CONV_K):
        start = SUBLANE - (CONV_K - 1) + j
        u = u + ubuf[start:start + rows, :] * cw_ref[j:j + 1, :]
    ubuf[0:SUBLANE, :] = ubuf[rows:rows + SUBLANE, :]
    qk = u * _sigmoid(u)
    upper = lax.broadcasted_iota(jnp.int32, (rows, HEAD_PAD), 1) >= MLSTM_QK_DIM
    for hd in range(MLSTM_HEADS):
        sl = slice(hd * HEAD_PAD, (hd + 1) * HEAD_PAD)
        swapped = pltpu.roll(qk[:, sl], MLSTM_QK_DIM, 1)
        qm_ref[:, sl] = jnp.where(upper, swapped * MLSTM_QK_DIM ** -0.5, 0.0).astype(BF16)
        kt_ref[sl, :] = qk[:, sl].T

    za = _dot(h, win_ref[:, Z_CQ:Z_QKM])
    cos = cos_ref[...]
    sin = sin_ref[...]
    cqn = _rms(za[:, Z_CQ:Z_CKV], qn_ref[...]).astype(BF16)
    ckvn = _rms(za[:, Z_CKV:Z_KR], kvn_ref[...]).astype(BF16)
    qa = _dot(cqn, wuq_ref[...])
    qb = _dot(cqn, wuqr_ref[...])
    kn = _dot(ckvn, wuk_ref[...])
    k_rope = za[:, Z_KR:Z_KRR] * cos + za[:, Z_KRR:Z_GATE] * sin
    vv = _dot(ckvn, wuv_ref[...])
    scale = Q_HEAD ** -0.5 * np.log2(np.e)
    ones_lane = lax.broadcasted_iota(jnp.int32, cos.shape, 1) >= V_HEAD_DIM
    for hd in range(MLA_HEADS):
        sl = slice(hd * HEAD_PAD, (hd + 1) * HEAD_PAD)
        q_ref[:, sl] = ((qa[:, sl] * cos + qb[:, sl] * sin) * scale).astype(BF16)
        k_ref[:, sl] = (kn[:, sl] + k_rope).astype(BF16)
        v_ref[:, sl] = jnp.where(ones_lane, 1.0, vv[:, sl]).astype(BF16)

    zb = _dot(h, win_ref[:, Z_VM:Z_WIDTH])
    vm_ref[...] = zb[:, :MLSTM_WIDTH].astype(BF16)
    om_ref[...] = zb[:, MLSTM_WIDTH:]
    gz = za[:, Z_GATE:Z_QKM] + gb_ref[...]
    lane = lax.broadcasted_iota(jnp.int32, gz.shape, 1)
    gc_ref[...] = jnp.where(lane < MLSTM_HEADS, gz, _log_sigmoid(gz))


def _proj(x, layer, w, cos, sin, seq_len):
    t = x.shape[0]
    rows = min(PROJ_ROWS, seq_len)

    def row_spec(width):
        return pl.BlockSpec((rows, width), lambda i: (i, 0))

    def const(shape):
        return pl.BlockSpec((None,) + shape, lambda i: (layer, 0, 0), pipeline_mode=pl.Buffered(1))

    out_widths = ((ATT_PAD, BF16), (ATT_PAD, BF16), (ATT_PAD, BF16), (MQK_PAD, BF16), None,
                  (MLSTM_WIDTH, BF16), (MLSTM_WIDTH, F32), (LANE, F32))
    kt_spec = pl.BlockSpec((MQK_PAD, rows), lambda i: (0, i))
    kt_shape = jax.ShapeDtypeStruct((MQK_PAD, t), F32)
    nbytes = (D_MODEL * Z_WIDTH * 2 + 2 * Q_LORA_RANK * ATT_PAD * 2 + 2 * KV_LORA_RANK * ATT_PAD * 2
              + rows * (Z_WIDTH * 4 + 4 * ATT_PAD * 4 + 2 * D_MODEL * 4 + 6 * MQK_PAD * 4)
              + 2 * rows * (MQK_PAD * 4 + sum(wd[0] * jnp.dtype(wd[1]).itemsize
                                              for wd in out_widths if wd is not None)))
    return pl.pallas_call(
        functools.partial(_proj_body, rows=rows, steps_per_seq=seq_len // rows),
        grid=(t // rows,),
        in_specs=[row_spec(D_MODEL), const((1, D_MODEL)), const((D_MODEL, Z_WIDTH)),
                  const((1, Q_LORA_RANK)), const((Q_LORA_RANK, ATT_PAD)), const((Q_LORA_RANK, ATT_PAD)),
                  const((1, KV_LORA_RANK)), const((KV_LORA_RANK, ATT_PAD)), const((KV_LORA_RANK, ATT_PAD)),
                  row_spec(LANE), row_spec(LANE), const((1, LANE)),
                  const((CONV_K, MQK_PAD)), const((1, MQK_PAD))],
        out_specs=[kt_spec if wd is None else row_spec(wd[0]) for wd in out_widths],
        out_shape=[kt_shape if wd is None else jax.ShapeDtypeStruct((t, wd[0]), wd[1])
                   for wd in out_widths],
        scratch_shapes=[pltpu.VMEM((rows + SUBLANE, MQK_PAD), F32)],
        compiler_params=pltpu.CompilerParams(
            dimension_semantics=("arbitrary",), vmem_limit_bytes=_vmem_limit(nbytes + (8 << 20))),
        name="in_proj",
    )(x, w["mix_norm"], w["w_in"], w["q_norm"], w["w_uq"], w["w_uq_rot"], w["kv_norm"],
      w["w_uk"], w["w_uv"], cos, sin, w["gate_bias"], w["conv_w"], w["conv_b"])


def _attn_body(q_ref, k_ref, v_ref, g_ref, o_ref, s_scr, p_scr, m_scr, acc_scr, fin_scr,
               *, blk, heads, nq):
    def rows(i):
        return pl.ds(pl.multiple_of(i * blk, blk), blk)

    def produce(hd, qi, j):
        sl = slice(hd * HEAD_PAD, (hd + 1) * HEAD_PAD)
        s_scr[hd] = lax.dot_general(q_ref[0, rows(qi), sl], k_ref[0, rows(j), sl],
                                    (((1,), (1,)), ((), ())), preferred_element_type=F32)

    half = blk // 2
    tri = (lax.broadcasted_iota(jnp.int32, (ATT_ROW_GROUP, ATT_ROW_GROUP), 1)
           <= lax.broadcasted_iota(jnp.int32, (ATT_ROW_GROUP, ATT_ROW_GROUP), 0))

    def consume(hd, j, diagonal=False):
        sl = slice(hd * HEAD_PAD, (hd + 1) * HEAD_PAD)
        for r0 in range(0, blk, ATT_ROW_GROUP):
            rg = slice(r0, r0 + ATT_ROW_GROUP)
            if diagonal:
                vis = r0 + ATT_ROW_GROUP
                ncol = half if r0 < half else blk
                own = jnp.where(tri, s_scr[hd, rg, r0:vis], -jnp.inf)
                s = own if r0 == 0 else jnp.concatenate([s_scr[hd, rg, :r0], own], axis=1)
            else:
                vis = ncol = blk
                s = s_scr[hd, rg]
            m_old = m_scr[hd, rg]
            m_new = jnp.maximum(m_old, jnp.max(s, axis=-1, keepdims=True))
            p_scr[hd, rg, :vis] = jnp.exp2(
                s - jnp.concatenate([m_new] * (vis // LANE), axis=1)).astype(BF16)
            if vis < ncol:
                p_scr[hd, rg, vis:ncol] = jnp.zeros((ATT_ROW_GROUP, ncol - vis), BF16)
            acc_scr[hd, rg] = jnp.exp2(m_old - m_new) * acc_scr[hd, rg]
            m_scr[hd, rg] = m_new
        if diagonal:
            top = pl.ds(pl.multiple_of(j * blk, blk), half)
            acc_scr[hd, :half] += _dot(p_scr[hd, :half, :half], v_ref[0, top, sl])
            acc_scr[hd, half:] += _dot(p_scr[hd, half:], v_ref[0, rows(j), sl])
        else:
            acc_scr[hd] += _dot(p_scr[hd], v_ref[0, rows(j), sl])

    def reset():
        m_scr[...] = jnp.full(m_scr.shape, -jnp.inf, F32)
        acc_scr[...] = jnp.zeros(acc_scr.shape, F32)

    def finish(pair, qi):
        lane = lax.broadcasted_iota(jnp.int32, (blk, HEAD_PAD), 1)
        normed = []
        for hd in (2 * pair, 2 * pair + 1):
            acc = fin_scr[hd]
            o = jnp.where(lane < V_HEAD_DIM, acc / pltpu.roll(acc, V_HEAD_DIM, 1), 0.0)
            ms = jnp.sum(o * o, axis=-1, keepdims=True) * (1.0 / V_HEAD_DIM)
            normed.append(o * lax.rsqrt(ms + EPS))
        sl = slice(pair * HEAD_PAD, (pair + 1) * HEAD_PAD)
        both = normed[0] + pltpu.roll(normed[1], V_HEAD_DIM, 1)
        o_ref[0, rows(qi), sl] = (both * g_ref[:, sl]).astype(BF16)

    reset()
    fin_scr[...] = jnp.ones(fin_scr.shape, F32)
    for hd in range(heads):
        produce(hd, 0, 0)

    def query_block(qi, carry):
        def step(j, c):
            for hd in range(heads):
                consume(hd, j)
                produce(hd, qi, j + 1)
            return c

        lax.fori_loop(0, qi, step, 0)
        prev = jnp.maximum(qi - 1, 0)
        nxt = jnp.minimum(qi + 1, nq - 1)
        for hd in range(heads):
            if hd % 2 == 0:
                finish(hd // 2, prev)
            consume(hd, qi, diagonal=True)
            produce(hd, nxt, 0)
        fin_scr[...] = acc_scr[...]
        reset()
        return carry

    lax.fori_loop(0, nq, query_block, 0)
    for pair in range(heads // 2):
        finish(pair, nq - 1)


def _attention(q, k, v, g, layer):
    b, s, _ = q.shape
    blk = min(ATT_BLOCK, s)
    assert blk % (2 * ATT_ROW_GROUP) == 0 and ATT_ROW_GROUP == LANE
    width = ATT_HEADS_PER_STEP * HEAD_PAD
    out_width = ATT_HEADS_PER_STEP * V_HEAD_DIM
    seq_spec = pl.BlockSpec((1, s, width), lambda bi, hi: (bi, 0, hi))
    return pl.pallas_call(
        functools.partial(_attn_body, blk=blk, heads=ATT_HEADS_PER_STEP, nq=s // blk),
        grid=(b, MLA_HEADS // ATT_HEADS_PER_STEP),
        in_specs=[seq_spec, seq_spec, seq_spec,
                  pl.BlockSpec((None, 1, out_width), lambda bi, hi: (layer, 0, hi))],
        out_specs=pl.BlockSpec((1, s, out_width), lambda bi, hi: (bi, 0, hi)),
        out_shape=jax.ShapeDtypeStruct((b, s, MLA_WIDTH), BF16),
        scratch_shapes=[pltpu.VMEM((ATT_HEADS_PER_STEP, blk, blk), F32),
                        pltpu.VMEM((ATT_HEADS_PER_STEP, blk, blk), BF16),
                        pltpu.VMEM((ATT_HEADS_PER_STEP, blk, HEAD_PAD), F32),
                        pltpu.VMEM((ATT_HEADS_PER_STEP, blk, HEAD_PAD), F32),
                        pltpu.VMEM((ATT_HEADS_PER_STEP, blk, HEAD_PAD), F32)],
        compiler_params=pltpu.CompilerParams(
            dimension_semantics=("parallel", "parallel"),
            vmem_limit_bytes=_vmem_limit(10 * s * width * 2 + ATT_HEADS_PER_STEP * 8 * blk * blk * 4
                                         + (8 << 20))),
        name="mla_attention",
    )(q, k, v, g)


def _chunk_scan(x, chunk, op, fill):
    row = lax.broadcasted_iota(jnp.int32, x.shape, 0) % chunk
    d = 1
    while d < chunk:
        x = op(x, jnp.where(row >= d, pltpu.roll(x, d, 0), fill))
        d *= 2
    return x


def _split3(x):
    hi = x.astype(BF16)
    r1 = x - hi.astype(F32)
    mid = r1.astype(BF16)
    lo = (r1 - mid.astype(F32)).astype(BF16)
    return jnp.concatenate([hi, mid, lo], axis=-1)


def _mlstm_body(qm_ref, kt_ref, vm_ref, om_ref, gc_ref, rep_ref, hn_ref, y_ref, cst, mst,
                *, rows, chunk):
    @pl.when(pl.program_id(1) == 0)
    def _():
        cst[...] = jnp.zeros(cst.shape, F32)
        mst[...] = jnp.zeros(mst.shape, F32)

    gc = gc_ref[0]
    lane = lax.broadcasted_iota(jnp.int32, gc.shape, 1)
    bcum = _chunk_scan(gc, chunk, jnp.add, 0.0)
    a_col = gc - pltpu.roll(bcum, LANE - MLSTM_HEADS, 1)
    a_max = _chunk_scan(a_col, chunk, jnp.maximum, -jnp.inf)
    a_row = a_col.T
    rep_all = _dot(_split3(jnp.where(lane < MLSTM_HEADS, a_max, bcum)), rep_ref[...])

    tri = (lax.broadcasted_iota(jnp.int32, (chunk, chunk), 1)
           <= lax.broadcasted_iota(jnp.int32, (chunk, chunk), 0))
    ones = jnp.ones((chunk, LANE), BF16)

    chunks = [slice(c * chunk, (c + 1) * chunk) for c in range(rows // chunk)]
    heads = range(MLSTM_HEADS)
    hs = [slice(hd * HEAD_PAD, (hd + 1) * HEAD_PAD) for hd in heads]
    vs = [slice(hd * MLSTM_V_DIM, (hd + 1) * MLSTM_V_DIM) for hd in heads]
    rep = [rep_all[:, 2 * hd * LANE:2 * (hd + 1) * LANE] for hd in heads]
    v_aug = [[jnp.concatenate([vm_ref[0, r, vs[hd]], ones], axis=-1) for r in chunks]
             for hd in heads]
    a_top = [[rep[hd][r, :LANE][chunk - 1:chunk, :] for r in chunks] for hd in heads]

    kv = [[_dot((kt_ref[hs[hd], r] * jnp.exp(a_row[hd:hd + 1, r] - a_top[hd][c])).astype(BF16),
                v_aug[hd][c]) for c, r in enumerate(chunks)] for hd in heads]

    states = []
    for hd in heads:
        c_aug = cst[hd]
        m_prev = mst[hd:hd + 1, :]
        entry = []
        for c, r in enumerate(chunks):
            entry.append((c_aug, m_prev))
            m_top = jnp.maximum(m_prev, a_top[hd][c])
            d_old = jnp.exp(m_prev - m_top)
            d_new = jnp.exp(a_top[hd][c] - m_top)
            c_aug = (jnp.concatenate([d_old, d_old], axis=1) * c_aug
                     + jnp.concatenate([d_new, d_new], axis=1) * kv[hd][c])
            m_prev = rep[hd][r, LANE:][chunk - 1:chunk, :] + m_top
        cst[hd] = c_aug
        mst[hd:hd + 1, :] = m_prev
        states.append(entry)

    for c, r in enumerate(chunks):
        for hd in heads:
            c_in, m_in = states[hd][c]
            amax_rep = rep[hd][r, :LANE]
            b_rep = rep[hd][r, LANE:]
            big_m = jnp.maximum(m_in, amax_rep)
            qc = qm_ref[0, r, hs[hd]]
            scores = _dot(qc, kt_ref[hs[hd], r].astype(BF16))
            p = jnp.exp(jnp.where(tri, a_row[hd:hd + 1, r] - big_m, -jnp.inf)) * scores
            e_inter = jnp.exp(m_in - big_m)
            qc_state = _dot(qc, c_in.astype(BF16))
            pv = _dot(p.astype(BF16), v_aug[hd][c])
            num = e_inter * qc_state[:, :MLSTM_V_DIM] + pv[:, :MLSTM_V_DIM]
            den = e_inter * qc_state[:, MLSTM_V_DIM:] + pv[:, MLSTM_V_DIM:]
            hout = num / jnp.maximum(jnp.abs(den), jnp.exp(-(b_rep + big_m)))
            hn = _rms(hout, hn_ref[:, vs[hd]])
            y_ref[0, r, vs[hd]] = (_sigmoid(om_ref[0, r, vs[hd]]) * hn).astype(BF16)


def _mlstm(qm, kt, vm, om, gc, w, layer):
    b, s, _ = qm.shape
    rows = min(MLSTM_ROWS, s)
    chunk = min(MLSTM_CHUNK, rows)
    steps = s // rows

    def seq_spec(width):
        return pl.BlockSpec((1, rows, width), lambda bi, ti: (bi, ti, 0))

    return pl.pallas_call(
        functools.partial(_mlstm_body, rows=rows, chunk=chunk),
        grid=(b, steps),
        in_specs=[seq_spec(MQK_PAD), pl.BlockSpec((MQK_PAD, rows), lambda bi, ti: (0, bi * steps + ti)),
                  seq_spec(MLSTM_WIDTH), seq_spec(MLSTM_WIDTH), seq_spec(LANE),
                  pl.BlockSpec((3 * LANE, 2 * MQK_PAD), lambda bi, ti: (0, 0)),
                  pl.BlockSpec((None, 1, MLSTM_WIDTH), lambda bi, ti: (layer, 0, 0))],
        out_specs=seq_spec(MLSTM_WIDTH),
        out_shape=jax.ShapeDtypeStruct((b, s, MLSTM_WIDTH), BF16),
        scratch_shapes=[pltpu.VMEM((MLSTM_HEADS, HEAD_PAD, 2 * LANE), F32),
                        pltpu.VMEM((SUBLANE, LANE), F32)],
        compiler_params=pltpu.CompilerParams(
            dimension_semantics=("parallel", "arbitrary"),
            vmem_limit_bytes=_vmem_limit(rows * 2 * MQK_PAD * 4 * 8 + (8 << 20))),
        name="mlstm",
    )(qm, kt, vm, om, gc, w["gate_rep"], w["mlstm_norm"])


def _gather_cols(w, src, coef):
    parts = []
    i, n = 0, len(src)
    while i < n:
        j = i + 1
        while j < n and coef[j] == coef[i] and (coef[i] == 0 or src[j] == src[j - 1] + 1):
            j += 1
        if coef[i] == 0:
            parts.append(jnp.zeros(w.shape[:-1] + (j - i,), w.dtype))
        else:
            piece = lax.slice_in_dim(w, int(src[i]), int(src[i]) + (j - i), axis=w.ndim - 1)
            parts.append(piece if coef[i] == 1 else piece * float(coef[i]))
        i = j
    return jnp.concatenate(parts, axis=-1)


def _pad_heads(n_heads, width, base=0, stride=None):
    stride = width if stride is None else stride
    src = np.zeros(n_heads * HEAD_PAD, np.int64)
    coef = np.zeros(n_heads * HEAD_PAD, np.float32)
    for hd in range(n_heads):
        src[hd * HEAD_PAD:hd * HEAD_PAD + width] = base + hd * stride + np.arange(width)
        coef[hd * HEAD_PAD:hd * HEAD_PAD + width] = 1.0
    return src, coef


def _qk_pairs(q_base, k_base):
    d = np.arange(MLSTM_QK_DIM)
    return np.concatenate([np.concatenate([q_base + hd * MLSTM_QK_DIM + d, k_base + hd * MLSTM_QK_DIM + d])
                           for hd in range(MLSTM_HEADS)])


def _rot_half_cols(base):
    half = QK_ROPE_DIM // 2
    src = np.concatenate([base + half + np.arange(half), base + np.arange(half)])
    coef = np.concatenate([-np.ones(half, np.float32), np.ones(half, np.float32)])
    return src, coef


def _prepare_weights(p):
    off = np.concatenate([[0], np.cumsum(IN_SPLITS)])
    o_cq, o_ckv, o_kr, o_qm, o_km, o_vm, o_om, o_im, o_fm = (int(v) for v in off[:-1])

    src = np.zeros(Z_WIDTH, np.int64)
    coef = np.zeros(Z_WIDTH, np.float32)

    def put(dst, s, c):
        src[dst:dst + len(s)] = s
        coef[dst:dst + len(s)] = c

    put(Z_CQ, o_cq + np.arange(Q_LORA_RANK), 1.0)
    put(Z_CKV, o_ckv + np.arange(KV_LORA_RANK), 1.0)
    put(Z_KR + QK_NOPE_DIM, o_kr + np.arange(QK_ROPE_DIM), 1.0)
    put(Z_KRR + QK_NOPE_DIM, *_rot_half_cols(o_kr))
    qk_src = _qk_pairs(o_qm, o_km)
    put(Z_QKM, qk_src, 1.0)
    put(Z_VM, o_vm + np.arange(MLSTM_WIDTH), 1.0)
    put(Z_OM, o_om + np.arange(MLSTM_WIDTH), 1.0)
    put(Z_GATE, o_im + np.arange(MLSTM_HEADS), 1.0)
    put(Z_GATE + MLSTM_HEADS, o_fm + np.arange(MLSTM_HEADS), 1.0)
    w_in = _gather_cols(p["w_in"], src, coef).astype(BF16)

    q_src, q_coef = _pad_heads(MLA_HEADS, Q_HEAD)
    r_src = np.zeros(ATT_PAD, np.int64)
    r_coef = np.zeros(ATT_PAD, np.float32)
    for hd in range(MLA_HEADS):
        s, c = _rot_half_cols(hd * Q_HEAD + QK_NOPE_DIM)
        lo = hd * HEAD_PAD + QK_NOPE_DIM
        r_src[lo:lo + QK_ROPE_DIM] = s
        r_coef[lo:lo + QK_ROPE_DIM] = c
    k_src, k_coef = _pad_heads(MLA_HEADS, QK_NOPE_DIM, 0, KV_HEAD)
    v_src, v_coef = _pad_heads(MLA_HEADS, V_HEAD_DIM, QK_NOPE_DIM, KV_HEAD)

    conv_src = _qk_pairs(0, MLSTM_QK_WIDTH)
    conv_coef = np.ones(MQK_PAD, np.float32)

    depth = p["w_in"].shape[0]

    gate_bias = jnp.concatenate(
        [p["b_igate"], p["b_fgate"], jnp.zeros((depth, LANE - 2 * MLSTM_HEADS), F32)], axis=-1)

    def row(a):
        return a[:, None, :]

    gate_rep = np.zeros((3, LANE, 2 * MQK_PAD), np.float32)
    for hd in range(MLSTM_HEADS):
        gate_rep[:, hd, 2 * hd * LANE:(2 * hd + 1) * LANE] = 1.0
        gate_rep[:, MLSTM_HEADS + hd, (2 * hd + 1) * LANE:(2 * hd + 2) * LANE] = 1.0

    out = {
        "gate_rep": jnp.asarray(gate_rep.reshape(3 * LANE, 2 * MQK_PAD), BF16),
        "w_in": w_in,
        "mix_norm": row(p["mix_norm"]),
        "q_norm": row(p["q_latent_norm"]),
        "kv_norm": row(p["kv_latent_norm"]),
        "w_uq": _gather_cols(p["w_uq"], q_src, q_coef).astype(BF16),
        "w_uq_rot": _gather_cols(p["w_uq"], r_src, r_coef).astype(BF16),
        "w_uk": _gather_cols(p["w_ukv"], k_src, k_coef).astype(BF16),
        "w_uv": _gather_cols(p["w_ukv"], v_src, v_coef).astype(BF16),
        "gate_bias": row(gate_bias),
        "att_norm": row(p["attn_head_norm"]),
        "conv_w": _gather_cols(p["conv_w"], conv_src, conv_coef),
        "conv_b": row(_gather_cols(p["conv_b"], conv_src, conv_coef)),
        "mlstm_norm": row(p["mlstm_head_norm"]),
        "w_out": p["w_out"].astype(BF16),
    }
    for name in ("ffn1", "ffn2"):
        out[name + "_norm"] = row(p[name + "_norm"])
        for mat in ("w_gate", "w_up", "w_down"):
            out[f"{name}_{mat}"] = p[f"{name}_{mat}"].astype(BF16)
    return out


def kernel(x, positions, ffn1_norm, ffn1_w_gate, ffn1_w_up, ffn1_w_down, mix_norm, w_in, q_latent_norm, w_uq, kv_latent_norm, w_ukv, conv_w, conv_b, b_igate, b_fgate, attn_head_norm, mlstm_head_norm, w_out, ffn2_norm, ffn2_w_gate, ffn2_w_up, ffn2_w_down, final_norm):
    b, s, d = x.shape
    depth = w_in.shape[0]
    w = _prepare_weights(dict(
        ffn1_norm=ffn1_norm, ffn1_w_gate=ffn1_w_gate, ffn1_w_up=ffn1_w_up, ffn1_w_down=ffn1_w_down,
        mix_norm=mix_norm, w_in=w_in, q_latent_norm=q_latent_norm, w_uq=w_uq,
        kv_latent_norm=kv_latent_norm, w_ukv=w_ukv, conv_w=conv_w, conv_b=conv_b,
        b_igate=b_igate, b_fgate=b_fgate, attn_head_norm=attn_head_norm,
        mlstm_head_norm=mlstm_head_norm, w_out=w_out, ffn2_norm=ffn2_norm,
        ffn2_w_gate=ffn2_w_gate, ffn2_w_up=ffn2_w_up, ffn2_w_down=ffn2_w_down))
    cos, sin = _rope_tables(positions)

    t = b * s
    xt = x.reshape(t, d)
    for layer in range(depth):
        xt = _ffn(xt, layer, w["ffn1_norm"], w["ffn1_w_gate"], w["ffn1_w_up"], w["ffn1_w_down"])
        q, k, v, qm, kt, vm, om, gc = _proj(xt, layer, w, cos, sin, s)
        seq = lambda a: a.reshape(b, s, a.shape[-1])
        y_att = _attention(seq(q), seq(k), seq(v), w["att_norm"], layer)
        y_mem = _mlstm(seq(qm), kt, seq(vm), seq(om), seq(gc), w, layer)
        xt = _ffn(xt, layer, w["ffn2_norm"], w["ffn2_w_gate"], w["ffn2_w_up"], w["ffn2_w_down"],
                  mix=(y_att.reshape(t, MLA_WIDTH), y_mem.reshape(t, MLSTM_WIDTH), w["w_out"]),
                  final_norm=final_norm.reshape(1, d) if layer == depth - 1 else None)
    return xt.reshape(b, s, d)
```

```python
import functools

import jax
import jax.numpy as jnp
import numpy as np
from jax import lax
from jax.experimental import pallas as pl
from jax.experimental.pallas import tpu as pltpu

F32 = jnp.float32
BF16 = jnp.bfloat16

LANE = 128
SUBLANE = 8
V7X_VMEM_BYTES = 64 * 1024 * 1024

D_MODEL = 1024
MLA_HEADS = 8
QK_NOPE_DIM = 64
QK_ROPE_DIM = 32
V_HEAD_DIM = 64
Q_LORA_RANK = 256
KV_LORA_RANK = 128
ROPE_THETA = 10000.0
MLSTM_HEADS = 4
MLSTM_QK_DIM = 64
MLSTM_V_DIM = 128
CONV_K = 4
D_FF = 2816
EPS = 1e-6

MLA_WIDTH = MLA_HEADS * V_HEAD_DIM
MLSTM_WIDTH = MLSTM_HEADS * MLSTM_V_DIM
MLSTM_QK_WIDTH = MLSTM_HEADS * MLSTM_QK_DIM
Q_HEAD = QK_NOPE_DIM + QK_ROPE_DIM
KV_HEAD = QK_NOPE_DIM + V_HEAD_DIM
IN_SPLITS = (Q_LORA_RANK, KV_LORA_RANK, QK_ROPE_DIM, MLSTM_QK_WIDTH, MLSTM_QK_WIDTH,
             MLSTM_WIDTH, MLSTM_WIDTH, MLSTM_HEADS, MLSTM_HEADS)

HEAD_PAD = LANE
ATT_PAD = MLA_HEADS * HEAD_PAD
MQK_PAD = MLSTM_HEADS * HEAD_PAD
MIX_WIDTH = MLA_WIDTH + MLSTM_WIDTH
assert 2 * MLSTM_QK_DIM == HEAD_PAD and 2 * V_HEAD_DIM == HEAD_PAD

Z_CQ = 0
Z_CKV = Z_CQ + Q_LORA_RANK
Z_KR = Z_CKV + KV_LORA_RANK
Z_KRR = Z_KR + HEAD_PAD
Z_GATE = Z_KRR + HEAD_PAD
Z_QKM = Z_GATE + LANE
Z_VM = Z_QKM + MQK_PAD
Z_OM = Z_VM + MLSTM_WIDTH
Z_WIDTH = Z_OM + MLSTM_WIDTH

FFN_ROWS = 1024
FFN_CHUNKS = ((0, 768), (768, 1536), (1536, 2304), (2304, 2816))
PROJ_ROWS = 1024
ATT_BLOCK = 512
ATT_HEADS_PER_STEP = 4
ATT_ROW_GROUP = 128
MLSTM_ROWS = 1024
MLSTM_CHUNK = 128
ROPE_ROWS = 2048


VMEM_RESERVED_BYTES = 4 << 20
VMEM_TEMPS_BYTES = 8 << 20
VMEM_DEFAULT_BYTES = 32 << 20


def _vmem_limit(estimate):
    return int(min(V7X_VMEM_BYTES - VMEM_RESERVED_BYTES,
                   max(estimate + VMEM_TEMPS_BYTES, VMEM_DEFAULT_BYTES)))


def _rms(x, g):
    ms = jnp.mean(x * x, axis=-1, keepdims=True)
    return x * lax.rsqrt(ms + EPS) * g


def _sigmoid(x):
    return 1.0 / (1.0 + jnp.exp(-x))


def _dot(a, b):
    return jnp.dot(a, b, preferred_element_type=F32)


def _rope_body(pos_ref, inv_ref, cos_ref, sin_ref):
    groups = LANE // QK_ROPE_DIM
    quarter = pos_ref.shape[0] // groups
    lane = lax.broadcasted_iota(jnp.int32, (quarter, LANE), 1)
    pos = pos_ref[(groups - 1) * quarter:, :].astype(F32)
    for k in range(groups - 2, -1, -1):
        pos = jnp.where(lane < (k + 1) * QK_ROPE_DIM, pos_ref[k * quarter:(k + 1) * quarter, :].astype(F32),
                        pos)
    ang = pos * inv_ref[...]
    cos, sin = jnp.cos(ang), jnp.sin(ang)
    rope = (lane >= QK_NOPE_DIM) & (lane < Q_HEAD)
    for k in range(groups):
        shift = (QK_NOPE_DIM - k * QK_ROPE_DIM) % LANE
        ck = cos if shift == 0 else pltpu.roll(cos, shift, 1)
        sk = sin if shift == 0 else pltpu.roll(sin, shift, 1)
        out = slice(k * quarter, (k + 1) * quarter)
        cos_ref[out, :] = jnp.where(lane < QK_NOPE_DIM, 1.0, jnp.where(rope, ck, 0.0))
        sin_ref[out, :] = jnp.where(rope, sk, 0.0)


def _rope_tables(positions):
    n = positions.size
    rows = min(ROPE_ROWS, n)
    inv = ROPE_THETA ** (-jnp.arange(0, QK_ROPE_DIM, 2, dtype=F32) / QK_ROPE_DIM)
    inv_row = jnp.tile(jnp.concatenate([inv, inv]), LANE // QK_ROPE_DIM).reshape(1, LANE)
    spec = pl.BlockSpec((rows, LANE), lambda i: (i, 0))
    return pl.pallas_call(
        _rope_body,
        grid=(n // rows,),
        in_specs=[pl.BlockSpec((rows, 1), lambda i: (i, 0)),
                  pl.BlockSpec((1, LANE), lambda i: (0, 0))],
        out_specs=[spec, spec],
        out_shape=[jax.ShapeDtypeStruct((n, LANE), F32)] * 2,
        name="rope_tables",
    )(positions.reshape(n, 1), inv_row)


def _ffn_body(*refs, mix, final):
    refs = list(refs)
    o_ref = refs.pop()
    x = refs.pop(0)[...]
    if mix:
        ya_ref, ym_ref, wo_ref = refs[:3]
        refs = refs[3:]
        y = jnp.concatenate([ya_ref[...], ym_ref[...]], axis=-1)
        x = x + _dot(y, wo_ref[...])
    g_ref, wg_ref, wu_ref, wd_ref = refs[:4]
    h = _rms(x, g_ref[...]).astype(BF16)
    y = None
    for c0, c1 in FFN_CHUNKS:
        gate = _dot(h, wg_ref[:, c0:c1])
        up = _dot(h, wu_ref[:, c0:c1])
        act = (gate * _sigmoid(gate) * up).astype(BF16)
        part = _dot(act, wd_ref[c0:c1, :])
        y = part if y is None else y + part
    out = x + 0.5 * y
    if final:
        out = _rms(out, refs[4][...])
    o_ref[...] = out


def _ffn(x, layer, norm, wg, wu, wd, mix=None, final_norm=None):
    t = x.shape[0]
    rows = min(FFN_ROWS, t)

    def row_spec(w):
        return pl.BlockSpec((rows, w), lambda i: (i, 0))

    def const(shape):
        return pl.BlockSpec((None,) + shape, lambda i: (layer, 0, 0), pipeline_mode=pl.Buffered(1))

    args, specs = [x], [row_spec(D_MODEL)]
    nbytes = 4 * rows * D_MODEL * 4 + 3 * D_MODEL * D_FF * 2
    if mix is not None:
        args += list(mix)
        specs += [row_spec(MLA_WIDTH), row_spec(MLSTM_WIDTH), const((MIX_WIDTH, D_MODEL))]
        nbytes += MIX_WIDTH * (D_MODEL * 2 + rows * 2 * 3)
    args += [norm, wg, wu, wd]
    specs += [const((1, D_MODEL)), const((D_MODEL, D_FF)), const((D_MODEL, D_FF)),
              const((D_FF, D_MODEL))]
    if final_norm is not None:
        args.append(final_norm)
        specs.append(pl.BlockSpec((1, D_MODEL), lambda i: (0, 0)))
    widest = max(c1 - c0 for c0, c1 in FFN_CHUNKS)
    nbytes += rows * widest * (4 + 4 + 4 + 2) + rows * D_MODEL * 4 * 3
    return pl.pallas_call(
        functools.partial(_ffn_body, mix=mix is not None, final=final_norm is not None),
        grid=(t // rows,),
        in_specs=specs,
        out_specs=row_spec(D_MODEL),
        out_shape=jax.ShapeDtypeStruct((t, D_MODEL), F32),
        compiler_params=pltpu.CompilerParams(
            dimension_semantics=("parallel",), vmem_limit_bytes=_vmem_limit(nbytes)),
        name="mix_ffn" if mix is not None else "ffn",
    )(*args)


def _log_sigmoid(x):
    return jnp.minimum(x, 0.0) - jnp.log1p(jnp.exp(-jnp.abs(x)))


def _proj_body(x_ref, g_ref, win_ref, qn_ref, wuq_ref, wuqr_ref, kvn_ref, wuk_ref, wuv_ref,
               cos_ref, sin_ref, gb_ref, cw_ref, cb_ref,
               q_ref, k_ref, v_ref, qm_ref, kt_ref, vm_ref, om_ref, gc_ref, ubuf,
               *, rows, steps_per_seq):
    h = _rms(x_ref[...], g_ref[...]).astype(BF16)

    @pl.when(pl.program_id(0) % steps_per_seq == 0)
    def _():
        ubuf[0:SUBLANE, :] = jnp.zeros((SUBLANE, MQK_PAD), F32)

    ubuf[SUBLANE:SUBLANE + rows, :] = _dot(h, win_ref[:, Z_QKM:Z_VM])
    u = cb_ref[...]
    for j in range(CONV_K):
        start = SUBLANE - (CONV_K - 1) + j
        u = u + ubuf[start:start + rows, :] * cw_ref[j:j + 1, :]
    ubuf[0:SUBLANE, :] = ubuf[rows:rows + SUBLANE, :]
    qk = u * _sigmoid(u)
    upper = lax.broadcasted_iota(jnp.int32, (rows, HEAD_PAD), 1) >= MLSTM_QK_DIM
    for hd in range(MLSTM_HEADS):
        sl = slice(hd * HEAD_PAD, (hd + 1) * HEAD_PAD)
        swapped = pltpu.roll(qk[:, sl], MLSTM_QK_DIM, 1)
        qm_ref[:, sl] = jnp.where(upper, swapped * MLSTM_QK_DIM ** -0.5, 0.0).astype(BF16)
        kt_ref[sl, :] = qk[:, sl].T

    za = _dot(h, win_ref[:, Z_CQ:Z_QKM])
    cos = cos_ref[...]
    sin = sin_ref[...]
    cqn = _rms(za[:, Z_CQ:Z_CKV], qn_ref[...]).astype(BF16)
    ckvn = _rms(za[:, Z_CKV:Z_KR], kvn_ref[...]).astype(BF16)
    qa = _dot(cqn, wuq_ref[...])
    qb = _dot(cqn, wuqr_ref[...])
    kn = _dot(ckvn, wuk_ref[...])
    k_rope = za[:, Z_KR:Z_KRR] * cos + za[:, Z_KRR:Z_GATE] * sin
    vv = _dot(ckvn, wuv_ref[...])
    scale = Q_HEAD ** -0.5 * np.log2(np.e)
    ones_lane = lax.broadcasted_iota(jnp.int32, cos.shape, 1) >= V_HEAD_DIM
    for hd in range(MLA_HEADS):
        sl = slice(hd * HEAD_PAD, (hd + 1) * HEAD_PAD)
        q_ref[:, sl] = ((qa[:, sl] * cos + qb[:, sl] * sin) * scale).astype(BF16)
        k_ref[:, sl] = (kn[:, sl] + k_rope).astype(BF16)
        v_ref[:, sl] = jnp.where(ones_lane, 1.0, vv[:, sl]).astype(BF16)

    zb = _dot(h, win_ref[:, Z_VM:Z_WIDTH])
    vm_ref[...] = zb[:, :MLSTM_WIDTH].astype(BF16)
    om_ref[...] = zb[:, MLSTM_WIDTH:]
    gz = za[:, Z_GATE:Z_QKM] + gb_ref[...]
    lane = lax.broadcasted_iota(jnp.int32, gz.shape, 1)
    gc_ref[...] = jnp.where(lane < MLSTM_HEADS, gz, _log_sigmoid(gz))


def _proj(x, layer, w, cos, sin, seq_len):
    t = x.shape[0]
    rows = min(PROJ_ROWS, seq_len)

    def row_spec(width):
        return pl.BlockSpec((rows, width), lambda i: (i, 0))

    def const(shape):
        return pl.BlockSpec((None,) + shape, lambda i: (layer, 0, 0), pipeline_mode=pl.Buffered(1))

    out_widths = ((ATT_PAD, BF16), (ATT_PAD, BF16), (ATT_PAD, BF16), (MQK_PAD, BF16), None,
                  (MLSTM_WIDTH, BF16), (MLSTM_WIDTH, F32), (LANE, F32))
    kt_spec = pl.BlockSpec((MQK_PAD, rows), lambda i: (0, i))
    kt_shape = jax.ShapeDtypeStruct((MQK_PAD, t), F32)
    nbytes = (D_MODEL * Z_WIDTH * 2 + 2 * Q_LORA_RANK * ATT_PAD * 2 + 2 * KV_LORA_RANK * ATT_PAD * 2
              + rows * (Z_WIDTH * 4 + 4 * ATT_PAD * 4 + 2 * D_MODEL * 4 + 6 * MQK_PAD * 4)
              + 2 * rows * (MQK_PAD * 4 + sum(wd[0] * jnp.dtype(wd[1]).itemsize
                                              for wd in out_widths if wd is not None)))
    return pl.pallas_call(
        functools.partial(_proj_body, rows=rows, steps_per_seq=seq_len // rows),
        grid=(t // rows,),
        in_specs=[row_spec(D_MODEL), const((1, D_MODEL)), const((D_MODEL, Z_WIDTH)),
                  const((1, Q_LORA_RANK)), const((Q_LORA_RANK, ATT_PAD)), const((Q_LORA_RANK, ATT_PAD)),
                  const((1, KV_LORA_RANK)), const((KV_LORA_RANK, ATT_PAD)), const((KV_LORA_RANK, ATT_PAD)),
                  row_spec(LANE), row_spec(LANE), const((1, LANE)),
                  const((CONV_K, MQK_PAD)), const((1, MQK_PAD))],
        out_specs=[kt_spec if wd is None else row_spec(wd[0]) for wd in out_widths],
        out_shape=[kt_shape if wd is None else jax.ShapeDtypeStruct((t, wd[0]), wd[1])
                   for wd in out_widths],
        scratch_shapes=[pltpu.VMEM((rows + SUBLANE, MQK_PAD), F32)],
        compiler_params=pltpu.CompilerParams(
            dimension_semantics=("arbitrary",), vmem_limit_bytes=_vmem_limit(nbytes)),
        name="in_proj",
    )(x, w["mix_norm"], w["w_in"], w["q_norm"], w["w_uq"], w["w_uq_rot"], w["kv_norm"],
      w["w_uk"], w["w_uv"], cos, sin, w["gate_bias"], w["conv_w"], w["conv_b"])


def _attn_body(q_ref, k_ref, v_ref, g_ref, o_ref, s_scr, p_scr, m_scr, acc_scr, fin_scr,
               *, blk, heads, nq):
    def rows(i):
        return pl.ds(pl.multiple_of(i * blk, blk), blk)

    def produce(hd, qi, j):
        sl = slice(hd * HEAD_PAD, (hd + 1) * HEAD_PAD)
        s_scr[hd] = lax.dot_general(q_ref[0, rows(qi), sl], k_ref[0, rows(j), sl],
                                    (((1,), (1,)), ((), ())), preferred_element_type=F32)

    half = blk // 2
    tri = (lax.broadcasted_iota(jnp.int32, (ATT_ROW_GROUP, ATT_ROW_GROUP), 1)
           <= lax.broadcasted_iota(jnp.int32, (ATT_ROW_GROUP, ATT_ROW_GROUP), 0))

    def consume(hd, j, diagonal=False):
        sl = slice(hd * HEAD_PAD, (hd + 1) * HEAD_PAD)
        for r0 in range(0, blk, ATT_ROW_GROUP):
            rg = slice(r0, r0 + ATT_ROW_GROUP)
            if diagonal:
                vis = r0 + ATT_ROW_GROUP
                ncol = half if r0 < half else blk
                own = jnp.where(tri, s_scr[hd, rg, r0:vis], -jnp.inf)
                s = own if r0 == 0 else jnp.concatenate([s_scr[hd, rg, :r0], own], axis=1)
            else:
                vis = ncol = blk
                s = s_scr[hd, rg]
            m_old = m_scr[hd, rg]
            m_new = jnp.maximum(m_old, jnp.max(s, axis=-1, keepdims=True))
            p_scr[hd, rg, :vis] = jnp.exp2(
                s - jnp.concatenate([m_new] * (vis // LANE), axis=1)).astype(BF16)
            if vis < ncol:
                p_scr[hd, rg, vis:ncol] = jnp.zeros((ATT_ROW_GROUP, ncol - vis), BF16)
            acc_scr[hd, rg] = jnp.exp2(m_old - m_new) * acc_scr[hd, rg]
            m_scr[hd, rg] = m_new
        if diagonal:
            top = pl.ds(pl.multiple_of(j * blk, blk), half)
            acc_scr[hd, :half] += _dot(p_scr[hd, :half, :half], v_ref[0, top, sl])
            acc_scr[hd, half:] += _dot(p_scr[hd, half:], v_ref[0, rows(j), sl])
        else:
            acc_scr[hd] += _dot(p_scr[hd], v_ref[0, rows(j), sl])

    def reset():
        m_scr[...] = jnp.full(m_scr.shape, -jnp.inf, F32)
        acc_scr[...] = jnp.zeros(acc_scr.shape, F32)

    def finish(pair, qi):
        lane = lax.broadcasted_iota(jnp.int32, (blk, HEAD_PAD), 1)
        normed = []
        for hd in (2 * pair, 2 * pair + 1):
            acc = fin_scr[hd]
            o = jnp.where(lane < V_HEAD_DIM, acc / pltpu.roll(acc, V_HEAD_DIM, 1), 0.0)
            ms = jnp.sum(o * o, axis=-1, keepdims=True) * (1.0 / V_HEAD_DIM)
            normed.append(o * lax.rsqrt(ms + EPS))
        sl = slice(pair * HEAD_PAD, (pair + 1) * HEAD_PAD)
        both = normed[0] + pltpu.roll(normed[1], V_HEAD_DIM, 1)
        o_ref[0, rows(qi), sl] = (both * g_ref[:, sl]).astype(BF16)

    reset()
    fin_scr[...] = jnp.ones(fin_scr.shape, F32)
    for hd in range(heads):
        produce(hd, 0, 0)

    def query_block(qi, carry):
        def step(j, c):
            for hd in range(heads):
                consume(hd, j)
                produce(hd, qi, j + 1)
            return c

        lax.fori_loop(0, qi, step, 0)
        prev = jnp.maximum(qi - 1, 0)
        nxt = jnp.minimum(qi + 1, nq - 1)
        for hd in range(heads):
            if hd % 2 == 0:
                finish(hd // 2, prev)
            consume(hd, qi, diagonal=True)
            produce(hd, nxt, 0)
        fin_scr[...] = acc_scr[...]
        reset()
        return carry

    lax.fori_loop(0, nq, query_block, 0)
    for pair in range(heads // 2):
        finish(pair, nq - 1)


def _attention(q, k, v, g, layer):
    b, s, _ = q.shape
    blk = min(ATT_BLOCK, s)
    assert blk % (2 * ATT_ROW_GROUP) == 0 and ATT_ROW_GROUP == LANE
    width = ATT_HEADS_PER_STEP * HEAD_PAD
    out_width = ATT_HEADS_PER_STEP * V_HEAD_DIM
    seq_spec = pl.BlockSpec((1, s, width), lambda bi, hi: (bi, 0, hi))
    return pl.pallas_call(
        functools.partial(_attn_body, blk=blk, heads=ATT_HEADS_PER_STEP, nq=s // blk),
        grid=(b, MLA_HEADS // ATT_HEADS_PER_STEP),
        in_specs=[seq_spec, seq_spec, seq_spec,
                  pl.BlockSpec((None, 1, out_width), lambda bi, hi: (layer, 0, hi))],
        out_specs=pl.BlockSpec((1, s, out_width), lambda bi, hi: (bi, 0, hi)),
        out_shape=jax.ShapeDtypeStruct((b, s, MLA_WIDTH), BF16),
        scratch_shapes=[pltpu.VMEM((ATT_HEADS_PER_STEP, blk, blk), F32),
                        pltpu.VMEM((ATT_HEADS_PER_STEP, blk, blk), BF16),
                        pltpu.VMEM((ATT_HEADS_PER_STEP, blk, HEAD_PAD), F32),
                        pltpu.VMEM((ATT_HEADS_PER_STEP, blk, HEAD_PAD), F32),
                        pltpu.VMEM((ATT_HEADS_PER_STEP, blk, HEAD_PAD), F32)],
        compiler_params=pltpu.CompilerParams(
            dimension_semantics=("parallel", "parallel"),
            vmem_limit_bytes=_vmem_limit(8 * s * width * 2 + ATT_HEADS_PER_STEP * blk * (6 * blk + 12 * HEAD_PAD))),
        name="mla_attention",
    )(q, k, v, g)


def _chunk_scan(x, chunk, op, fill):
    row = lax.broadcasted_iota(jnp.int32, x.shape, 0) % chunk
    d = 1
    while d < chunk:
        x = op(x, jnp.where(row >= d, pltpu.roll(x, d, 0), fill))
        d *= 2
    return x


def _split3(x):
    hi = x.astype(BF16)
    r1 = x - hi.astype(F32)
    mid = r1.astype(BF16)
    lo = (r1 - mid.astype(F32)).astype(BF16)
    return jnp.concatenate([hi, mid, lo], axis=-1)


def _mlstm_body(qm_ref, kt_ref, vm_ref, om_ref, gc_ref, rep_ref, hn_ref, y_ref, cst, mst,
                *, rows, chunk):
    @pl.when(pl.program_id(1) == 0)
    def _():
        cst[...] = jnp.zeros(cst.shape, F32)
        mst[...] = jnp.zeros(mst.shape, F32)

    gc = gc_ref[0]
    lane = lax.broadcasted_iota(jnp.int32, gc.shape, 1)
    bcum = _chunk_scan(gc, chunk, jnp.add, 0.0)
    a_col = gc - pltpu.roll(bcum, LANE - MLSTM_HEADS, 1)
    a_max = _chunk_scan(a_col, chunk, jnp.maximum, -jnp.inf)
    a_row = a_col.T
    rep_all = _dot(_split3(jnp.where(lane < MLSTM_HEADS, a_max, bcum)), rep_ref[...])

    tri = (lax.broadcasted_iota(jnp.int32, (chunk, chunk), 1)
           <= lax.broadcasted_iota(jnp.int32, (chunk, chunk), 0))
    ones = jnp.ones((chunk, LANE), BF16)

    chunks = [slice(c * chunk, (c + 1) * chunk) for c in range(rows // chunk)]
    heads = range(MLSTM_HEADS)
    hs = [slice(hd * HEAD_PAD, (hd + 1) * HEAD_PAD) for hd in heads]
    vs = [slice(hd * MLSTM_V_DIM, (hd + 1) * MLSTM_V_DIM) for hd in heads]
    rep = [rep_all[:, 2 * hd * LANE:2 * (hd + 1) * LANE] for hd in heads]
    v_aug = [[jnp.concatenate([vm_ref[0, r, vs[hd]], ones], axis=-1) for r in chunks]
             for hd in heads]
    a_top = [[rep[hd][r, :LANE][chunk - 1:chunk, :] for r in chunks] for hd in heads]

    kv = [[_dot((kt_ref[hs[hd], r] * jnp.exp(a_row[hd:hd + 1, r] - a_top[hd][c])).astype(BF16),
                v_aug[hd][c]) for c, r in enumerate(chunks)] for hd in heads]

    states = []
    for hd in heads:
        c_aug = cst[hd]
        m_prev = mst[hd:hd + 1, :]
        entry = []
        for c, r in enumerate(chunks):
            entry.append((c_aug, m_prev))
            m_top = jnp.maximum(m_prev, a_top[hd][c])
            d_old = jnp.exp(m_prev - m_top)
            d_new = jnp.exp(a_top[hd][c] - m_top)
            c_aug = (jnp.concatenate([d_old, d_old], axis=1) * c_aug
                     + jnp.concatenate([d_new, d_new], axis=1) * kv[hd][c])
            m_prev = rep[hd][r, LANE:][chunk - 1:chunk, :] + m_top
        cst[hd] = c_aug
        mst[hd:hd + 1, :] = m_prev
        states.append(entry)

    for c, r in enumerate(chunks):
        for hd in heads:
            c_in, m_in = states[hd][c]
            amax_rep = rep[hd][r, :LANE]
            b_rep = rep[hd][r, LANE:]
            big_m = jnp.maximum(m_in, amax_rep)
            qc = qm_ref[0, r, hs[hd]]
            scores = _dot(qc, kt_ref[hs[hd], r].astype(BF16))
            p = jnp.exp(jnp.where(tri, a_row[hd:hd + 1, r] - big_m, -jnp.inf)) * scores
            e_inter = jnp.exp(m_in - big_m)
            qc_state = _dot(qc, c_in.astype(BF16))
            pv = _dot(p.astype(BF16), v_aug[hd][c])
            num = e_inter * qc_state[:, :MLSTM_V_DIM] + pv[:, :MLSTM_V_DIM]
            den = e_inter * qc_state[:, MLSTM_V_DIM:] + pv[:, MLSTM_V_DIM:]
            hout = num / jnp.maximum(jnp.abs(den), jnp.exp(-(b_rep + big_m)))
            hn = _rms(hout, hn_ref[:, vs[hd]])
            y_ref[0, r, vs[hd]] = (_sigmoid(om_ref[0, r, vs[hd]]) * hn).astype(BF16)


def _mlstm(qm, kt, vm, om, gc, w, layer):
    b, s, _ = qm.shape
    rows = min(MLSTM_ROWS, s)
    chunk = min(MLSTM_CHUNK, rows)
    steps = s // rows

    def seq_spec(width):
        return pl.BlockSpec((1, rows, width), lambda bi, ti: (bi, ti, 0))

    return pl.pallas_call(
        functools.partial(_mlstm_body, rows=rows, chunk=chunk),
        grid=(b, steps),
        in_specs=[seq_spec(MQK_PAD), pl.BlockSpec((MQK_PAD, rows), lambda bi, ti: (0, bi * steps + ti)),
                  seq_spec(MLSTM_WIDTH), seq_spec(MLSTM_WIDTH), seq_spec(LANE),
                  pl.BlockSpec((3 * LANE, 2 * MQK_PAD), lambda bi, ti: (0, 0)),
                  pl.BlockSpec((None, 1, MLSTM_WIDTH), lambda bi, ti: (layer, 0, 0))],
        out_specs=seq_spec(MLSTM_WIDTH),
        out_shape=jax.ShapeDtypeStruct((b, s, MLSTM_WIDTH), BF16),
        scratch_shapes=[pltpu.VMEM((MLSTM_HEADS, HEAD_PAD, 2 * LANE), F32),
                        pltpu.VMEM((SUBLANE, LANE), F32)],
        compiler_params=pltpu.CompilerParams(
            dimension_semantics=("parallel", "arbitrary"),
            vmem_limit_bytes=_vmem_limit(rows * (2 * (2 + 4 + 2 + 4 + 2) * MQK_PAD + 2 * 4 * LANE
                                                 + 3 * 4 * 2 * MQK_PAD))),
        name="mlstm",
    )(qm, kt, vm, om, gc, w["gate_rep"], w["mlstm_norm"])


def _gather_cols(w, src, coef):
    parts = []
    i, n = 0, len(src)
    while i < n:
        j = i + 1
        while j < n and coef[j] == coef[i] and (coef[i] == 0 or src[j] == src[j - 1] + 1):
            j += 1
        if coef[i] == 0:
            parts.append(jnp.zeros(w.shape[:-1] + (j - i,), w.dtype))
        else:
            piece = lax.slice_in_dim(w, int(src[i]), int(src[i]) + (j - i), axis=w.ndim - 1)
            parts.append(piece if coef[i] == 1 else piece * float(coef[i]))
        i = j
    return jnp.concatenate(parts, axis=-1)


def _pad_heads(n_heads, width, base=0, stride=None):
    stride = width if stride is None else stride
    src = np.zeros(n_heads * HEAD_PAD, np.int64)
    coef = np.zeros(n_heads * HEAD_PAD, np.float32)
    for hd in range(n_heads):
        src[hd * HEAD_PAD:hd * HEAD_PAD + width] = base + hd * stride + np.arange(width)
        coef[hd * HEAD_PAD:hd * HEAD_PAD + width] = 1.0
    return src, coef


def _qk_pairs(q_base, k_base):
    d = np.arange(MLSTM_QK_DIM)
    return np.concatenate([np.concatenate([q_base + hd * MLSTM_QK_DIM + d, k_base + hd * MLSTM_QK_DIM + d])
                           for hd in range(MLSTM_HEADS)])


def _rot_half_cols(base):
    half = QK_ROPE_DIM // 2
    src = np.concatenate([base + half + np.arange(half), base + np.arange(half)])
    coef = np.concatenate([-np.ones(half, np.float32), np.ones(half, np.float32)])
    return src, coef


def _prepare_weights(p):
    off = np.concatenate([[0], np.cumsum(IN_SPLITS)])
    o_cq, o_ckv, o_kr, o_qm, o_km, o_vm, o_om, o_im, o_fm = (int(v) for v in off[:-1])

    src = np.zeros(Z_WIDTH, np.int64)
    coef = np.zeros(Z_WIDTH, np.float32)

    def put(dst, s, c):
        src[dst:dst + len(s)] = s
        coef[dst:dst + len(s)] = c

    put(Z_CQ, o_cq + np.arange(Q_LORA_RANK), 1.0)
    put(Z_CKV, o_ckv + np.arange(KV_LORA_RANK), 1.0)
    put(Z_KR + QK_NOPE_DIM, o_kr + np.arange(QK_ROPE_DIM), 1.0)
    put(Z_KRR + QK_NOPE_DIM, *_rot_half_cols(o_kr))
    qk_src = _qk_pairs(o_qm, o_km)
    put(Z_QKM, qk_src, 1.0)
    put(Z_VM, o_vm + np.arange(MLSTM_WIDTH), 1.0)
    put(Z_OM, o_om + np.arange(MLSTM_WIDTH), 1.0)
    put(Z_GATE, o_im + np.arange(MLSTM_HEADS), 1.0)
    put(Z_GATE + MLSTM_HEADS, o_fm + np.arange(MLSTM_HEADS), 1.0)
    w_in = _gather_cols(p["w_in"], src, coef).astype(BF16)

    q_src, q_coef = _pad_heads(MLA_HEADS, Q_HEAD)
    r_src = np.zeros(ATT_PAD, np.int64)
    r_coef = np.zeros(ATT_PAD, np.float32)
    for hd in range(MLA_HEADS):
        s, c = _rot_half_cols(hd * Q_HEAD + QK_NOPE_DIM)
        lo = hd * HEAD_PAD + QK_NOPE_DIM
        r_src[lo:lo + QK_ROPE_DIM] = s
        r_coef[lo:lo + QK_ROPE_DIM] = c
    k_src, k_coef = _pad_heads(MLA_HEADS, QK_NOPE_DIM, 0, KV_HEAD)
    v_src, v_coef = _pad_heads(MLA_HEADS, V_HEAD_DIM, QK_NOPE_DIM, KV_HEAD)

    conv_src = _qk_pairs(0, MLSTM_QK_WIDTH)
    conv_coef = np.ones(MQK_PAD, np.float32)

    depth = p["w_in"].shape[0]

    gate_bias = jnp.concatenate(
        [p["b_igate"], p["b_fgate"], jnp.zeros((depth, LANE - 2 * MLSTM_HEADS), F32)], axis=-1)

    def row(a):
        return a[:, None, :]

    gate_rep = np.zeros((3, LANE, 2 * MQK_PAD), np.float32)
    for hd in range(MLSTM_HEADS):
        gate_rep[:, hd, 2 * hd * LANE:(2 * hd + 1) * LANE] = 1.0
        gate_rep[:, MLSTM_HEADS + hd, (2 * hd + 1) * LANE:(2 * hd + 2) * LANE] = 1.0

    out = {
        "gate_rep": jnp.asarray(gate_rep.reshape(3 * LANE, 2 * MQK_PAD), BF16),
        "w_in": w_in,
        "mix_norm": row(p["mix_norm"]),
        "q_norm": row(p["q_latent_norm"]),
        "kv_norm": row(p["kv_latent_norm"]),
        "w_uq": _gather_cols(p["w_uq"], q_src, q_coef).astype(BF16),
        "w_uq_rot": _gather_cols(p["w_uq"], r_src, r_coef).astype(BF16),
        "w_uk": _gather_cols(p["w_ukv"], k_src, k_coef).astype(BF16),
        "w_uv": _gather_cols(p["w_ukv"], v_src, v_coef).astype(BF16),
        "gate_bias": row(gate_bias),
        "att_norm": row(p["attn_head_norm"]),
        "conv_w": _gather_cols(p["conv_w"], conv_src, conv_coef),
        "conv_b": row(_gather_cols(p["conv_b"], conv_src, conv_coef)),
        "mlstm_norm": row(p["mlstm_head_norm"]),
        "w_out": p["w_out"].astype(BF16),
    }
    for name in ("ffn1", "ffn2"):
        out[name + "_norm"] = row(p[name + "_norm"])
        for mat in ("w_gate", "w_up", "w_down"):
            out[f"{name}_{mat}"] = p[f"{name}_{mat}"].astype(BF16)
    return out


def kernel(x, positions, ffn1_norm, ffn1_w_gate, ffn1_w_up, ffn1_w_down, mix_norm, w_in, q_latent_norm, w_uq, kv_latent_norm, w_ukv, conv_w, conv_b, b_igate, b_fgate, attn_head_norm, mlstm_head_norm, w_out, ffn2_norm, ffn2_w_gate, ffn2_w_up, ffn2_w_down, final_norm):
    b, s, d = x.shape
    depth = w_in.shape[0]
    w = _prepare_weights(dict(
        ffn1_norm=ffn1_norm, ffn1_w_gate=ffn1_w_gate, ffn1_w_up=ffn1_w_up, ffn1_w_down=ffn1_w_down,
        mix_norm=mix_norm, w_in=w_in, q_latent_norm=q_latent_norm, w_uq=w_uq,
        kv_latent_norm=kv_latent_norm, w_ukv=w_ukv, conv_w=conv_w, conv_b=conv_b,
        b_igate=b_igate, b_fgate=b_fgate, attn_head_norm=attn_head_norm,
        mlstm_head_norm=mlstm_head_norm, w_out=w_out, ffn2_norm=ffn2_norm,
        ffn2_w_gate=ffn2_w_gate, ffn2_w_up=ffn2_w_up, ffn2_w_down=ffn2_w_down))
    cos, sin = _rope_tables(positions)

    t = b * s
    xt = x.reshape(t, d)
    for layer in range(depth):
        xt = _ffn(xt, layer, w["ffn1_norm"], w["ffn1_w_gate"], w["ffn1_w_up"], w["ffn1_w_down"])
        q, k, v, qm, kt, vm, om, gc = _proj(xt, layer, w, cos, sin, s)
        seq = lambda a: a.reshape(b, s, a.shape[-1])
        y_att = _attention(seq(q), seq(k), seq(v), w["att_norm"], layer)
        y_mem = _mlstm(seq(qm), kt, seq(vm), seq(om), seq(gc), w, layer)
        xt = _ffn(xt, layer, w["ffn2_norm"], w["ffn2_w_gate"], w["ffn2_w_up"], w["ffn2_w_down"],
                  mix=(y_att.reshape(t, MLA_WIDTH), y_mem.reshape(t, MLSTM_WIDTH), w["w_out"]),
                  final_norm=final_norm.reshape(1, d) if layer == depth - 1 else None)
    return xt.reshape(b, s, d)
```
